```python
import math
import jax, jax.numpy as jnp
from jax import lax
import numpy as np

D_MODEL = 2048
BATCH = 1
SEQ = 8192
DEPTH = 4
DEC_BATCH = 8
DEC_SEQ = 4096
PAST_LEN = 128

N_MIXERS = 3
N_A_LAYERS = (DEPTH + 2) // 3
N_B_LAYERS = (DEPTH + 1) // 3
N_C_LAYERS = DEPTH // 3
N_MEM = 256
D_FF = 5504
CHUNK = 128
A_WIDTH = D_MODEL
A_GROUPS = 8
A_GROUP_DIM = A_WIDTH // A_GROUPS
B_HEADS = 8
B_HEAD_DIM = D_MODEL // (2 * B_HEADS)
B_VALUE_DIM = 2 * B_HEAD_DIM
ROT_DIM = B_HEAD_DIM // 4
ROPE_THETA = 500000.0
Q_BLOCK = 128
C_GROUPS = 8
C_GROUP_DIM = D_MODEL // C_GROUPS
X_HEADS = 4
X_HEAD_DIM = 128
X_WIDTH = X_HEADS * X_HEAD_DIM
NORM_EPS = 1e-6
SUBLN_EPS = 1e-5
LN_EPS = 1e-5

kernel_name = "hybrid_bidir_gmlp_diffattn_fnet_encoder"

F32 = jnp.float32


def rmsnorm(x, g, eps=NORM_EPS):
    xf = x.astype(F32)
    y = xf * lax.rsqrt(jnp.mean(xf * xf, axis=-1, keepdims=True) + eps)
    return (y * g.astype(F32)).astype(x.dtype)


def layernorm(x, g, b, eps=LN_EPS):
    xf = x.astype(F32)
    mu = jnp.mean(xf, axis=-1, keepdims=True)
    xc = xf - mu
    var = jnp.mean(xc * xc, axis=-1, keepdims=True)
    return (xc * lax.rsqrt(var + eps) * g.astype(F32) + b.astype(F32)).astype(x.dtype)


def swiglu(x, w_gate, w_up, w_down):
    return (jax.nn.silu(x @ w_gate) * (x @ w_up)) @ w_down


def chunked_gmlp(xn, w_in, ln_g, ln_b, w_s, b_s, w_out):
    b, s, _ = xn.shape
    h = jax.nn.gelu(xn @ w_in, approximate=False)
    u, v = jnp.split(h, 2, axis=-1)
    v = layernorm(v, ln_g, ln_b)
    v = v.reshape(b, s // CHUNK, CHUNK, A_GROUPS, A_GROUP_DIM)
    mixed = jnp.einsum('gpq,bcqgd->bcpgd', w_s, v) + b_s.T[:, :, None]
    return (u * mixed.reshape(b, s, A_WIDTH)) @ w_out


def rotary_tables(s):
    inv_freq = 1.0 / (ROPE_THETA ** (jnp.arange(0, ROT_DIM, 2, dtype=F32) / ROT_DIM))
    ang = jnp.arange(s, dtype=F32)[:, None] * inv_freq[None, :]
    return jnp.cos(ang), jnp.sin(ang)


def partial_rotary(x, cos, sin):
    xf = x.astype(F32)
    half = ROT_DIM // 2
    x1 = xf[..., :half]
    x2 = xf[..., half:ROT_DIM]
    c = cos[None, :, None, None, :]
    sn = sin[None, :, None, None, :]
    out = jnp.concatenate([x1 * c - x2 * sn, x2 * c + x1 * sn, xf[..., ROT_DIM:]], axis=-1)
    return out.astype(x.dtype)


def diff_attention(xn, w_qkv, lam_q1, lam_k1, lam_q2, lam_k2, subln_g, w_out, layer_idx):
    b, s, _ = xn.shape
    q, k, v = jnp.split(xn @ w_qkv, 3, axis=-1)
    q = q.reshape(b, s, B_HEADS, 2, B_HEAD_DIM)
    k = k.reshape(b, s, B_HEADS, 2, B_HEAD_DIM)
    v = v.reshape(b, s, B_HEADS, B_VALUE_DIM)
    cos, sin = rotary_tables(s)
    q = partial_rotary(q, cos, sin) * (B_HEAD_DIM ** -0.5)
    k = partial_rotary(k, cos, sin)
    lam_init = 0.8 - 0.6 * math.exp(-0.3 * layer_idx)
    lam = (jnp.exp(jnp.sum(lam_q1.astype(F32) * lam_k1.astype(F32)))
           - jnp.exp(jnp.sum(lam_q2.astype(F32) * lam_k2.astype(F32))) + lam_init)
    q_blocks = jnp.moveaxis(q.reshape(b, s // Q_BLOCK, Q_BLOCK, B_HEADS, 2, B_HEAD_DIM), 1, 0)

    def attend(qb):
        scores = jnp.einsum('bqhcd,bkhcd->bhcqk', qb, k, preferred_element_type=F32)
        p = jax.nn.softmax(scores, axis=-1)
        diff = (p[:, :, 0] - lam * p[:, :, 1]).astype(v.dtype)
        return jnp.einsum('bhqk,bkhe->bqhe', diff, v)

    o = lax.map(attend, q_blocks)
    o = jnp.moveaxis(o, 0, 1).reshape(b, s, B_HEADS, B_VALUE_DIM)
    o = rmsnorm(o, subln_g, eps=SUBLN_EPS) * (1.0 - lam_init)
    return o.reshape(b, s, D_MODEL) @ w_out


def fourier_mix(xn, w_out):
    b, s, _ = xn.shape
    g = xn.reshape(b, s, C_GROUPS, C_GROUP_DIM).astype(F32)
    mixed = jnp.fft.fft2(g, axes=(1, 3), norm="ortho").real
    return mixed.reshape(b, s, D_MODEL).astype(xn.dtype) @ w_out


def memory_cross_attention(xn, mem_n, w_q, w_kv, w_out):
    b, s, _ = xn.shape
    m = mem_n.shape[1]
    q = (xn @ w_q).reshape(b, s, X_HEADS, X_HEAD_DIM) * (X_HEAD_DIM ** -0.5)
    k, v = jnp.split(mem_n @ w_kv, 2, axis=-1)
    k = k.reshape(b, m, X_HEADS, X_HEAD_DIM)
    v = v.reshape(b, m, X_HEADS, X_HEAD_DIM)
    scores = jnp.einsum('bqhd,bmhd->bhqm', q, k, preferred_element_type=F32)
    p = jax.nn.softmax(scores, axis=-1).astype(v.dtype)
    o = jnp.einsum('bhqm,bmhd->bqhd', p, v).reshape(b, s, X_WIDTH)
    return o @ w_out


def encoder_trunk(x, mem, p):
    for i in range(DEPTH):
        kind, slot = i % N_MIXERS, i // N_MIXERS
        h = rmsnorm(x, p["ln_pre"][i, 0])
        f = swiglu(h, p["ffn_w_gate"][i, 0], p["ffn_w_up"][i, 0], p["ffn_w_down"][i, 0])
        x = x + 0.5 * rmsnorm(f, p["ln_post"][i, 0])
        h = rmsnorm(x, p["ln_pre"][i, 1])
        if kind == 0:
            m = chunked_gmlp(h, p["a_w_in"][slot], p["a_ln_g"][slot], p["a_ln_b"][slot],
                             p["a_w_s"][slot], p["a_b_s"][slot], p["a_w_out"][slot])
        elif kind == 1:
            m = diff_attention(h, p["b_w_qkv"][slot], p["b_lam_q1"][slot], p["b_lam_k1"][slot],
                               p["b_lam_q2"][slot], p["b_lam_k2"][slot], p["b_subln"][slot],
                               p["b_w_out"][slot], i)
        else:
            m = fourier_mix(h, p["c_w_out"][slot])
        x = x + rmsnorm(m, p["ln_post"][i, 1])
        h = rmsnorm(x, p["ln_pre"][i, 2])
        mem_n = rmsnorm(mem, p["ln_mem"][i])
        c = memory_cross_attention(h, mem_n, p["x_w_q"][i], p["x_w_kv"][i], p["x_w_out"][i])
        x = x + rmsnorm(c, p["ln_post"][i, 2])
        h = rmsnorm(x, p["ln_pre"][i, 3])
        f = swiglu(h, p["ffn_w_gate"][i, 1], p["ffn_w_up"][i, 1], p["ffn_w_down"][i, 1])
        x = x + 0.5 * rmsnorm(f, p["ln_post"][i, 3])
    return x


def _normal(key, shape, scale):
    return jax.random.normal(key, shape, F32) * scale


def setup_inputs(seed: int = 0) -> dict:
    key = jax.random.key(seed)
    ks = jax.random.split(key, 27)
    D = D_MODEL
    return {
        "x_prompt": _normal(ks[0], (BATCH, SEQ, D), 1.0),
        "x_sample": _normal(ks[1], (DEC_BATCH, DEC_SEQ, D), 1.0),
        "mem_prompt": _normal(ks[2], (BATCH, N_MEM, D), 1.0),
        "mem_sample": _normal(ks[3], (DEC_BATCH, N_MEM, D), 1.0),
        "ln_pre": 1.0 + _normal(ks[4], (DEPTH, 4, D), 0.02),
        "ln_post": 1.0 + _normal(ks[5], (DEPTH, 4, D), 0.02),
        "ln_mem": 1.0 + _normal(ks[6], (DEPTH, D), 0.02),
        "ffn_w_gate": _normal(ks[7], (DEPTH, 2, D, D_FF), D ** -0.5),
        "ffn_w_up": _normal(ks[8], (DEPTH, 2, D, D_FF), D ** -0.5),
        "ffn_w_down": _normal(ks[9], (DEPTH, 2, D_FF, D), D_FF ** -0.5),
        "a_w_in": _normal(ks[10], (N_A_LAYERS, D, 2 * A_WIDTH), D ** -0.5),
        "a_ln_g": 1.0 + _normal(ks[11], (N_A_LAYERS, A_WIDTH), 0.02),
        "a_ln_b": _normal(ks[12], (N_A_LAYERS, A_WIDTH), 0.02),
        "a_w_s": _normal(ks[13], (N_A_LAYERS, A_GROUPS, CHUNK, CHUNK), CHUNK ** -0.5),
        "a_b_s": 1.0 + _normal(ks[14], (N_A_LAYERS, A_GROUPS, CHUNK), 0.02),
        "a_w_out": _normal(ks[15], (N_A_LAYERS, A_WIDTH, D), A_WIDTH ** -0.5),
        "b_w_qkv": _normal(ks[16], (N_B_LAYERS, D, 3 * D), D ** -0.5),
        "b_lam_q1": _normal(ks[17], (N_B_LAYERS, B_HEAD_DIM), 0.1),
        "b_lam_k1": _normal(ks[18], (N_B_LAYERS, B_HEAD_DIM), 0.1),
        "b_lam_q2": _normal(ks[19], (N_B_LAYERS, B_HEAD_DIM), 0.1),
        "b_lam_k2": _normal(ks[20], (N_B_LAYERS, B_HEAD_DIM), 0.1),
        "b_subln": 1.0 + _normal(ks[21], (N_B_LAYERS, B_VALUE_DIM), 0.02),
        "b_w_out": _normal(ks[22], (N_B_LAYERS, D, D), D ** -0.5),
        "c_w_out": _normal(ks[23], (N_C_LAYERS, D, D), D ** -0.5),
        "x_w_q": _normal(ks[24], (DEPTH, D, X_WIDTH), D ** -0.5),
        "x_w_kv": _normal(ks[25], (DEPTH, D, 2 * X_WIDTH), D ** -0.5),
        "x_w_out": _normal(ks[26], (DEPTH, X_WIDTH, D), X_WIDTH ** -0.5),
    }


def reference(x_prompt, x_sample, mem_prompt, mem_sample, ln_pre, ln_post, ln_mem,
              ffn_w_gate, ffn_w_up, ffn_w_down, a_w_in, a_ln_g, a_ln_b, a_w_s, a_b_s, a_w_out,
              b_w_qkv, b_lam_q1, b_lam_k1, b_lam_q2, b_lam_k2, b_subln, b_w_out, c_w_out,
              x_w_q, x_w_kv, x_w_out):
    params = {
        "ln_pre": ln_pre, "ln_post": ln_post, "ln_mem": ln_mem,
        "ffn_w_gate": ffn_w_gate, "ffn_w_up": ffn_w_up, "ffn_w_down": ffn_w_down,
        "a_w_in": a_w_in, "a_ln_g": a_ln_g, "a_ln_b": a_ln_b, "a_w_s": a_w_s, "a_b_s": a_b_s,
        "a_w_out": a_w_out,
        "b_w_qkv": b_w_qkv, "b_lam_q1": b_lam_q1, "b_lam_k1": b_lam_k1, "b_lam_q2": b_lam_q2,
        "b_lam_k2": b_lam_k2, "b_subln": b_subln, "b_w_out": b_w_out,
        "c_w_out": c_w_out,
        "x_w_q": x_w_q, "x_w_kv": x_w_kv, "x_w_out": x_w_out,
    }
    y_prompt = encoder_trunk(x_prompt, mem_prompt, params)
    y_sample = encoder_trunk(x_sample, mem_sample, params)
    return (y_prompt, y_sample)
```

```python
import functools
import math

import jax
import jax.numpy as jnp
from jax import lax
from jax.experimental import pallas as pl
from jax.experimental.pallas import tpu as pltpu

F32 = jnp.float32
BF16 = jnp.bfloat16

D = 2048
D_FF = 5504
LANE = 128
FF_PAD = 5632
CHUNK = 128
A_GROUPS = 8
A_GROUP_DIM = D // A_GROUPS
B_HEADS = 8
B_HEAD_DIM = 128
B_VALUE_DIM = 256
ROT_DIM = 32
ROPE_THETA = 500000.0
C_GROUPS = 8
C_GROUP_DIM = D // C_GROUPS
X_HEADS = 4
X_HEAD_DIM = 128
X_WIDTH = X_HEADS * X_HEAD_DIM
N_MEM = 256
NORM_EPS = 1e-6
SUBLN_EPS = 1e-5
LN_EPS = 1e-5
MIB = 1024 * 1024
VMEM_CAP_MIB = 60

FFN_TM, FFN_TF = 512, 512
ROW_TM = 512
ROW_CHUNK = 128
ATT_TQ, ATT_TK = 512, 512
DFT_TM, DFT_TK = 1024, 2048


def _params(semantics, vmem_mib):
    return pltpu.CompilerParams(dimension_semantics=semantics,
                                vmem_limit_bytes=min(vmem_mib, VMEM_CAP_MIB) * MIB)


def _row_chunks(n_rows, fn):
    rc = min(ROW_CHUNK, n_rows)

    def body(c, carry):
        fn(pl.ds(pl.multiple_of(c * rc, rc), rc))
        return carry

    lax.fori_loop(0, n_rows // rc, body, 0)


def _rms(x, g, eps):
    return x * lax.rsqrt(jnp.mean(x * x, axis=-1, keepdims=True) + eps) * g


def _norm_rows_to(h_ref, x_ref, g_ref):
    def fn(rs):
        h_ref[rs, :] = _rms(x_ref[rs, :], g_ref[...], NORM_EPS).astype(BF16)
    _row_chunks(x_ref.shape[0], fn)


def _postnorm_residual_to(o_ref, x_ref, y_ref, g_ref, scale):
    def fn(rs):
        o_ref[rs, :] = x_ref[rs, :] + scale * _rms(y_ref[rs, :], g_ref[...], NORM_EPS)
    _row_chunks(x_ref.shape[0], fn)


def _ffn_kernel(x_ref, gpre_ref, wg_ref, wu_ref, wd_ref, gpost_ref, o_ref, h_ref, acc_ref):
    f = pl.program_id(1)

    @pl.when(f == 0)
    def _():
        _norm_rows_to(h_ref, x_ref, gpre_ref)
        acc_ref[...] = jnp.zeros_like(acc_ref)

    h = h_ref[...]
    gate = jnp.dot(h, wg_ref[...], preferred_element_type=F32)
    up = jnp.dot(h, wu_ref[...], preferred_element_type=F32)
    a = (gate * jax.nn.sigmoid(gate) * up).astype(BF16)
    acc_ref[...] += jnp.dot(a, wd_ref[...], preferred_element_type=F32)

    @pl.when(f == pl.num_programs(1) - 1)
    def _():
        _postnorm_residual_to(o_ref, x_ref, acc_ref, gpost_ref, 0.5)


def _ffn(x, g_pre, wg, wu, wd, g_post):
    n = x.shape[0]
    tm = min(FFN_TM, n)
    tf = FFN_TF
    fp = wg.shape[1]
    vmem = (2 * 2 * tm * D * 4 + tm * D * (2 + 4) + 2 * 3 * D * tf * 2 + 3 * tm * tf * 4) // MIB + 6
    return pl.pallas_call(
        _ffn_kernel,
        grid=(n // tm, fp // tf),
        in_specs=[
            pl.BlockSpec((tm, D), lambda i, f: (i, 0)),
            pl.BlockSpec((1, D), lambda i, f: (0, 0)),
            pl.BlockSpec((D, tf), lambda i, f: (0, f)),
            pl.BlockSpec((D, tf), lambda i, f: (0, f)),
            pl.BlockSpec((tf, D), lambda i, f: (f, 0)),
            pl.BlockSpec((1, D), lambda i, f: (0, 0)),
        ],
        out_specs=pl.BlockSpec((tm, D), lambda i, f: (i, 0)),
        out_shape=jax.ShapeDtypeStruct((n, D), F32),
        scratch_shapes=[pltpu.VMEM((tm, D), BF16), pltpu.VMEM((tm, D), F32)],
        compiler_params=_params(("parallel", "arbitrary"), vmem),
        name="ffn",
    )(x, g_pre, wg, wu, wd, g_post)


def _norm_mm_kernel(x_ref, g_ref, w_ref, o_ref, h_ref):
    _norm_rows_to(h_ref, x_ref, g_ref)
    o_ref[...] = jnp.dot(h_ref[...], w_ref[...], preferred_element_type=F32).astype(o_ref.dtype)


def _norm_mm(x, g, w):
    n, n_out = x.shape[0], w.shape[1]
    tm = min(ROW_TM, n)
    tn = min(1024, n_out)
    vmem = (2 * tm * D * 4 + tm * D * 2 + 2 * D * tn * 2 + 2 * tm * tn * 2 + tm * tn * 4) // MIB + 6
    return pl.pallas_call(
        _norm_mm_kernel,
        grid=(n_out // tn, n // tm),
        in_specs=[
            pl.BlockSpec((tm, D), lambda j, i: (i, 0)),
            pl.BlockSpec((1, D), lambda j, i: (0, 0)),
            pl.BlockSpec((D, tn), lambda j, i: (0, j)),
        ],
        out_specs=pl.BlockSpec((tm, tn), lambda j, i: (i, j)),
        out_shape=jax.ShapeDtypeStruct((n, n_out), BF16),
        scratch_shapes=[pltpu.VMEM((tm, D), BF16)],
        compiler_params=_params(("parallel", "parallel"), vmem),
        name="norm_mm",
    )(x, g, w)


def _proj_res_kernel(x_ref, a_ref, w_ref, g_ref, o_ref, y_ref):
    y_ref[...] = jnp.dot(a_ref[...], w_ref[...], preferred_element_type=F32)
    _postnorm_residual_to(o_ref, x_ref, y_ref, g_ref, 1.0)


def _proj_res(x, a, w, g_post):
    n, k = a.shape
    tm = min(ROW_TM, n)
    vmem = (2 * 2 * tm * D * 4 + 2 * tm * k * 2 + 2 * k * D * 2 + tm * D * 4) // MIB + 6
    return pl.pallas_call(
        _proj_res_kernel,
        grid=(n // tm,),
        in_specs=[
            pl.BlockSpec((tm, D), lambda i: (i, 0)),
            pl.BlockSpec((tm, k), lambda i: (i, 0)),
            pl.BlockSpec((k, D), lambda i: (0, 0)),
            pl.BlockSpec((1, D), lambda i: (0, 0)),
        ],
        out_specs=pl.BlockSpec((tm, D), lambda i: (i, 0)),
        out_shape=jax.ShapeDtypeStruct((n, D), F32),
        scratch_shapes=[pltpu.VMEM((tm, D), F32)],
        compiler_params=_params(("parallel",), vmem),
        name="proj_res",
    )(x, a, w, g_post)


def _gelu(y):
    return 0.5 * y * (1.0 + lax.erf(y * (0.5 ** 0.5)))


def _gmlp_in_kernel(x_ref, g_ref, w_ref, lg_ref, lb_ref, o_ref, h_ref, y_ref):
    j = pl.program_id(0)
    _norm_rows_to(h_ref, x_ref, g_ref)
    y_ref[...] = jnp.dot(h_ref[...], w_ref[...], preferred_element_type=F32)

    @pl.when(j == 0)
    def _():
        def fn(rs):
            o_ref[rs, :] = _gelu(y_ref[rs, :]).astype(BF16)
        _row_chunks(x_ref.shape[0], fn)

    @pl.when(j == 1)
    def _():
        def fn(rs):
            v = _gelu(y_ref[rs, :])
            vc = v - jnp.mean(v, axis=-1, keepdims=True)
            var = jnp.mean(vc * vc, axis=-1, keepdims=True)
            o_ref[rs, :] = (vc * lax.rsqrt(var + LN_EPS) * lg_ref[...] + lb_ref[...]).astype(BF16)
        _row_chunks(x_ref.shape[0], fn)


def _gmlp_in(x, g_pre, w_in, ln_g, ln_b):
    n = x.shape[0]
    tm = min(ROW_TM, n)
    vmem = (2 * tm * D * 4 + tm * D * 2 + 2 * D * D * 2 + 2 * tm * D * 2 + tm * D * 4) // MIB + 6
    return pl.pallas_call(
        _gmlp_in_kernel,
        grid=(2, n // tm),
        in_specs=[
            pl.BlockSpec((tm, D), lambda j, i: (i, 0)),
            pl.BlockSpec((1, D), lambda j, i: (0, 0)),
            pl.BlockSpec((D, D), lambda j, i: (0, j)),
            pl.BlockSpec((1, D), lambda j, i: (0, 0)),
            pl.BlockSpec((1, D), lambda j, i: (0, 0)),
        ],
        out_specs=pl.BlockSpec((None, tm, D), lambda j, i: (j, i, 0)),
        out_shape=jax.ShapeDtypeStruct((2, n, D), BF16),
        scratch_shapes=[pltpu.VMEM((tm, D), BF16), pltpu.VMEM((tm, D), F32)],
        compiler_params=_params(("parallel", "parallel"), vmem),
        name="gmlp_in",
    )(x, g_pre, w_in, ln_g, ln_b)


def _gmlp_out_kernel(x_ref, u_ref, v_ref, ws_ref, bs_ref, w_ref, g_ref, o_ref, a_ref, y_ref):
    def fn(rs):
        for g in range(A_GROUPS):
            cs = slice(g * A_GROUP_DIM, (g + 1) * A_GROUP_DIM)
            mixed = jnp.dot(ws_ref[g], v_ref[rs, cs], preferred_element_type=F32) + bs_ref[:, g:g + 1]
            a_ref[rs, cs] = (u_ref[rs, cs].astype(F32) * mixed).astype(BF16)
    _row_chunks(x_ref.shape[0], fn)
    y_ref[...] = jnp.dot(a_ref[...], w_ref[...], preferred_element_type=F32)
    _postnorm_residual_to(o_ref, x_ref, y_ref, g_ref, 1.0)


def _gmlp_out(x, uv, w_s, b_s_t, w_out, g_post):
    n = x.shape[0]
    tm = min(ROW_TM, n)
    assert ROW_CHUNK == CHUNK and tm % CHUNK == 0
    vmem = (2 * 2 * tm * D * 4 + 2 * 2 * tm * D * 2 + 2 * D * D * 2 + tm * D * (2 + 4)) // MIB + 6
    return pl.pallas_call(
        _gmlp_out_kernel,
        grid=(n // tm,),
        in_specs=[
            pl.BlockSpec((tm, D), lambda i: (i, 0)),
            pl.BlockSpec((None, tm, D), lambda i: (0, i, 0)),
            pl.BlockSpec((None, tm, D), lambda i: (1, i, 0)),
            pl.BlockSpec((A_GROUPS, CHUNK, CHUNK), lambda i: (0, 0, 0)),
            pl.BlockSpec((CHUNK, A_GROUPS), lambda i: (0, 0)),
            pl.BlockSpec((D, D), lambda i: (0, 0)),
            pl.BlockSpec((1, D), lambda i: (0, 0)),
        ],
        out_specs=pl.BlockSpec((tm, D), lambda i: (i, 0)),
        out_shape=jax.ShapeDtypeStruct((n, D), F32),
        scratch_shapes=[pltpu.VMEM((tm, D), BF16), pltpu.VMEM((tm, D), F32)],
        compiler_params=_params(("parallel",), vmem),
        name="gmlp_out",
    )(x, uv, uv, w_s, b_s_t, w_out, g_post)


def _rotary_tables(s):
    half = ROT_DIM // 2
    inv_freq = 1.0 / (ROPE_THETA ** (jnp.arange(0, ROT_DIM, 2, dtype=F32) / ROT_DIM))
    ang = jnp.arange(s, dtype=F32)[:, None] * inv_freq[None, :]
    cos, sin = jnp.cos(ang), jnp.sin(ang)
    rest = B_HEAD_DIM - ROT_DIM
    ct = jnp.concatenate([cos, cos, jnp.ones((s, rest), F32)], axis=1)
    sa = jnp.concatenate([-sin, jnp.zeros((s, half + rest), F32)], axis=1)
    sb = jnp.concatenate([jnp.zeros((s, half), F32), sin, jnp.zeros((s, rest), F32)], axis=1)
    return ct, sa, sb


def _qkv_kernel(x_ref, g_ref, w_ref, ct_ref, sa_ref, sb_ref, o_ref, h_ref, y_ref):
    j = pl.program_id(0)
    _norm_rows_to(h_ref, x_ref, g_ref)
    y_ref[...] = jnp.dot(h_ref[...], w_ref[...], preferred_element_type=F32)

    @pl.when(j == 2)
    def _():
        def fn(rs):
            o_ref[rs, :] = y_ref[rs, :].astype(BF16)
        _row_chunks(x_ref.shape[0], fn)

    @pl.when(j < 2)
    def _():
        scale = jnp.where(j == 0, B_HEAD_DIM ** -0.5, 1.0).astype(F32)

        def fn(rs):
            ct, sa, sb = ct_ref[rs, :], sa_ref[rs, :], sb_ref[rs, :]
            for c in range(D // LANE):
                cs = slice(c * LANE, (c + 1) * LANE)
                yc = y_ref[rs, cs]
                rot = (yc * ct + pltpu.roll(yc, LANE - ROT_DIM // 2, 1) * sa
                       + pltpu.roll(yc, ROT_DIM // 2, 1) * sb)
                o_ref[rs, cs] = (rot * scale).astype(BF16)
        _row_chunks(x_ref.shape[0], fn)


def _qkv(x, g_pre, w_qkv, tables, seq):
    n = x.shape[0]
    tm = min(ROW_TM, seq)
    pos_tiles = seq // tm
    ct, sa, sb = tables
    vmem = (2 * tm * D * 4 + tm * D * 2 + 2 * D * D * 2 + 2 * tm * D * 2 + tm * D * 4) // MIB + 8
    tab = pl.BlockSpec((tm, LANE), lambda j, i: (i % pos_tiles, 0))
    return pl.pallas_call(
        _qkv_kernel,
        grid=(3, n // tm),
        in_specs=[
            pl.BlockSpec((tm, D), lambda j, i: (i, 0)),
            pl.BlockSpec((1, D), lambda j, i: (0, 0)),
            pl.BlockSpec((D, D), lambda j, i: (0, j)),
            tab, tab, tab,
        ],
        out_specs=pl.BlockSpec((None, tm, D), lambda j, i: (j, i, 0)),
        out_shape=jax.ShapeDtypeStruct((3, n, D), BF16),
        scratch_shapes=[pltpu.VMEM((tm, D), BF16), pltpu.VMEM((tm, D), F32)],
        compiler_params=_params(("parallel", "parallel"), vmem),
        name="qkv",
    )(x, g_pre, w_qkv, ct, sa, sb)


def _diffattn_kernel(q_ref, k_ref, v_ref, lq1_ref, lk1_ref, lq2_ref, lk2_ref, sg_ref, o_ref,
                     m_ref, l_ref, acc_ref, *, tk, lam_init):
    seq = k_ref.shape[0]
    lam = (jnp.exp(jnp.sum(lq1_ref[...] * lk1_ref[...], axis=-1, keepdims=True))
           - jnp.exp(jnp.sum(lq2_ref[...] * lk2_ref[...], axis=-1, keepdims=True)) + lam_init)
    m_ref[...] = jnp.full_like(m_ref, -jnp.inf)
    l_ref[...] = jnp.zeros_like(l_ref)
    acc_ref[...] = jnp.zeros_like(acc_ref)

    def body(t, carry):
        ks = pl.ds(pl.multiple_of(t * tk, tk), tk)
        v = v_ref[ks, :]
        for c in range(2):
            cs = slice(c * B_HEAD_DIM, (c + 1) * B_HEAD_DIM)
            s = lax.dot_general(q_ref[:, cs], k_ref[ks, cs], (((1,), (1,)), ((), ())),
                                preferred_element_type=F32)
            m_old = m_ref[c]
            m_new = jnp.maximum(m_old, jnp.max(s, axis=-1, keepdims=True))
            alpha = jnp.exp(m_old - m_new)
            p = jnp.exp(s - m_new)
            l_ref[c] = alpha * l_ref[c] + jnp.sum(p, axis=-1, keepdims=True)
            acc_ref[c] = alpha * acc_ref[c] + jnp.dot(p.astype(BF16), v, preferred_element_type=F32)
            m_ref[c] = m_new
        return carry

    lax.fori_loop(0, seq // tk, body, 0)
    o = acc_ref[0] / l_ref[0] - lam * (acc_ref[1] / l_ref[1])
    o_ref[...] = (_rms(o, sg_ref[...], SUBLN_EPS) * (1.0 - lam_init)).astype(BF16)


def _diffattn(qkv, lam_q1, lam_k1, lam_q2, lam_k2, subln_g, batch, seq, layer_idx):
    n = batch * seq
    tq = min(ATT_TQ, seq)
    tk = min(ATT_TK, seq)
    q_tiles = seq // tq
    hw = 2 * B_HEAD_DIM
    lam_init = 0.8 - 0.6 * math.exp(-0.3 * layer_idx)
    vec = pl.BlockSpec((1, B_HEAD_DIM), lambda b, h, i: (0, 0))
    vmem = (2 * 2 * seq * hw * 2 + 4 * tq * hw * 2 + 2 * tq * hw * 4 + 6 * tq * tk * 4) // MIB + 8
    return pl.pallas_call(
        functools.partial(_diffattn_kernel, tk=tk, lam_init=lam_init),
        grid=(batch, B_HEADS, q_tiles),
        in_specs=[
            pl.BlockSpec((None, tq, hw), lambda b, h, i: (0, b * q_tiles + i, h)),
            pl.BlockSpec((None, seq, hw), lambda b, h, i: (1, b, h)),
            pl.BlockSpec((None, seq, hw), lambda b, h, i: (2, b, h)),
            vec, vec, vec, vec,
            pl.BlockSpec((1, B_VALUE_DIM), lambda b, h, i: (0, 0)),
        ],
        out_specs=pl.BlockSpec((tq, hw), lambda b, h, i: (b * q_tiles + i, h)),
        out_shape=jax.ShapeDtypeStruct((n, D), BF16),
        scratch_shapes=[pltpu.VMEM((2, tq, 1), F32), pltpu.VMEM((2, tq, 1), F32),
                        pltpu.VMEM((2, tq, B_VALUE_DIM), F32)],
        compiler_params=_params(("parallel", "parallel", "arbitrary"), vmem),
        name="diffattn",
    )(qkv, qkv, qkv, lam_q1, lam_k1, lam_q2, lam_k2, subln_g)


def _dft_tables(n):
    idx = jnp.arange(n, dtype=jnp.int32)
    ang = ((idx[:, None] * idx[None, :]) % n).astype(F32) * (2.0 * math.pi / n)
    return jnp.cos(ang), jnp.sin(ang)


def _fourier_chan_kernel(x_ref, g_ref, cs_ref, o_ref, h_ref):
    _norm_rows_to(h_ref, x_ref, g_ref)
    for g in range(C_GROUPS):
        cs = slice(g * C_GROUP_DIM, (g + 1) * C_GROUP_DIM)
        ab = jnp.dot(h_ref[:, cs], cs_ref[...], preferred_element_type=F32)
        o_ref[0, :, cs] = ab[:, :C_GROUP_DIM].astype(BF16)
        o_ref[1, :, cs] = ab[:, C_GROUP_DIM:].astype(BF16)


def _fourier_chan(x, g_pre, chan_cs, batch, seq):
    tm = min(ROW_TM, seq)
    tiles = seq // tm
    vmem = (2 * tm * D * 4 + tm * D * 2 + 2 * 2 * tm * D * 2) // MIB + 8
    return pl.pallas_call(
        _fourier_chan_kernel,
        grid=(batch, tiles),
        in_specs=[
            pl.BlockSpec((tm, D), lambda b, i: (b * tiles + i, 0)),
            pl.BlockSpec((1, D), lambda b, i: (0, 0)),
            pl.BlockSpec((C_GROUP_DIM, 2 * C_GROUP_DIM), lambda b, i: (0, 0)),
        ],
        out_specs=pl.BlockSpec((None, 2, tm, D), lambda b, i: (b, 0, i, 0)),
        out_shape=jax.ShapeDtypeStruct((batch, 2, seq, D), BF16),
        scratch_shapes=[pltpu.VMEM((tm, D), BF16)],
        compiler_params=_params(("parallel", "parallel"), vmem),
        name="fourier_chan",
    )(x, g_pre, chan_cs)


def _fourier_pos_kernel(t_ref, ab_ref, o_ref, acc_ref, *, norm):
    k = pl.program_id(2)

    @pl.when(k == 0)
    def _():
        acc_ref[...] = jnp.zeros_like(acc_ref)

    acc_ref[...] += jnp.dot(t_ref[...], ab_ref[...], preferred_element_type=F32)

    @pl.when(k == pl.num_programs(2) - 1)
    def _():
        def fn(rs):
            o_ref[rs, :] = (acc_ref[rs, :] * norm).astype(BF16)
        _row_chunks(acc_ref.shape[0], fn)


def _fourier_pos(pos_table, ab, batch, seq):
    tm = min(DFT_TM, seq)
    tk = min(DFT_TK, 2 * seq)
    tiles = seq // tm
    ab2 = ab.reshape(batch * 2 * seq, D)
    k_tiles = 2 * seq // tk
    vmem = (2 * tm * tk * 2 + 2 * tk * D * 2 + tm * D * 4 + 2 * tm * D * 2) // MIB + 8
    return pl.pallas_call(
        functools.partial(_fourier_pos_kernel, norm=(seq * C_GROUP_DIM) ** -0.5),
        grid=(batch, tiles, k_tiles),
        in_specs=[
            pl.BlockSpec((tm, tk), lambda b, i, k: (i, k)),
            pl.BlockSpec((tk, D), lambda b, i, k: (b * k_tiles + k, 0)),
        ],
        out_specs=pl.BlockSpec((tm, D), lambda b, i, k: (b * tiles + i, 0)),
        out_shape=jax.ShapeDtypeStruct((batch * seq, D), BF16),
        scratch_shapes=[pltpu.VMEM((tm, D), F32)],
        compiler_params=_params(("parallel", "parallel", "arbitrary"), vmem),
        name="fourier_pos",
    )(pos_table, ab2)


def _xattn_kernel(x_ref, gpre_ref, wq_ref, kv_ref, wo_ref, gpost_ref, o_ref, h_ref, q_ref, a_ref, y_ref):
    _norm_rows_to(h_ref, x_ref, gpre_ref)
    q_ref[...] = (jnp.dot(h_ref[...], wq_ref[...], preferred_element_type=F32)
                  * (X_HEAD_DIM ** -0.5)).astype(BF16)

    def fn(rs):
        for h in range(X_HEADS):
            cs = slice(h * X_HEAD_DIM, (h + 1) * X_HEAD_DIM)
            vs = slice(X_WIDTH + h * X_HEAD_DIM, X_WIDTH + (h + 1) * X_HEAD_DIM)
            s = lax.dot_general(q_ref[rs, cs], kv_ref[:, cs], (((1,), (1,)), ((), ())),
                                preferred_element_type=F32)
            e = jnp.exp(s - jnp.max(s, axis=-1, keepdims=True))
            p = e / jnp.sum(e, axis=-1, keepdims=True)
            a_ref[rs, cs] = jnp.dot(p.astype(BF16), kv_ref[:, vs], preferred_element_type=F32).astype(BF16)
    _row_chunks(x_ref.shape[0], fn)
    y_ref[...] = jnp.dot(a_ref[...], wo_ref[...], preferred_element_type=F32)
    _postnorm_residual_to(o_ref, x_ref, y_ref, gpost_ref, 1.0)


def _xattn(x, g_pre, w_q, kv, w_out, g_post, seq):
    n = x.shape[0]
    tm = min(ROW_TM, seq)
    tiles = seq // tm
    vmem = (2 * 2 * tm * D * 4 + tm * D * (2 + 4) + 4 * D * X_WIDTH * 2 + 2 * tm * X_WIDTH * 2) // MIB + 8
    return pl.pallas_call(
        _xattn_kernel,
        grid=(n // tm,),
        in_specs=[
            pl.BlockSpec((tm, D), lambda i: (i, 0)),
            pl.BlockSpec((1, D), lambda i: (0, 0)),
            pl.BlockSpec((D, X_WIDTH), lambda i: (0, 0)),
            pl.BlockSpec((N_MEM, 2 * X_WIDTH), lambda i: (i // tiles, 0)),
            pl.BlockSpec((X_WIDTH, D), lambda i: (0, 0)),
            pl.BlockSpec((1, D), lambda i: (0, 0)),
        ],
        out_specs=pl.BlockSpec((tm, D), lambda i: (i, 0)),
        out_shape=jax.ShapeDtypeStruct((n, D), F32),
        scratch_shapes=[pltpu.VMEM((tm, D), BF16), pltpu.VMEM((tm, X_WIDTH), BF16),
                        pltpu.VMEM((tm, X_WIDTH), BF16), pltpu.VMEM((tm, D), F32)],
        compiler_params=_params(("parallel",), vmem),
        name="xattn",
    )(x, g_pre, w_q, kv, w_out, g_post)


def _prepare(p):
    pad = FF_PAD - D_FF
    w = dict(p)
    w["ffn_w_gate"] = jnp.pad(p["ffn_w_gate"].astype(BF16), ((0, 0), (0, 0), (0, 0), (0, pad)))
    w["ffn_w_up"] = jnp.pad(p["ffn_w_up"].astype(BF16), ((0, 0), (0, 0), (0, 0), (0, pad)))
    w["ffn_w_down"] = jnp.pad(p["ffn_w_down"].astype(BF16), ((0, 0), (0, 0), (0, pad), (0, 0)))
    for name in ("a_w_in", "a_w_s", "a_w_out", "b_w_qkv", "b_w_out", "c_w_out", "x_w_q", "x_w_kv", "x_w_out"):
        w[name] = p[name].astype(BF16)
    w["a_b_s_t"] = jnp.swapaxes(p["a_b_s"], 1, 2)
    cc, sc = _dft_tables(C_GROUP_DIM)
    w["chan_cs"] = jnp.concatenate([cc, sc], axis=1).astype(BF16)
    return w


def _row(v):
    return v.reshape(1, -1)


def _trunk(x3, mem3, w, depth):
    batch, seq, _ = x3.shape
    x = x3.reshape(batch * seq, D)
    mem = mem3.reshape(batch * N_MEM, D)
    n_mixers = 3
    rot = pos_table = None
    for i in range(depth):
        kind, slot = i % n_mixers, i // n_mixers
        pre, post = w["ln_pre"][i], w["ln_post"][i]
        x = _ffn(x, _row(pre[0]), w["ffn_w_gate"][i, 0], w["ffn_w_up"][i, 0], w["ffn_w_down"][i, 0], _row(post[0]))
        if kind == 0:
            uv = _gmlp_in(x, _row(pre[1]), w["a_w_in"][slot], _row(w["a_ln_g"][slot]), _row(w["a_ln_b"][slot]))
            x = _gmlp_out(x, uv, w["a_w_s"][slot], w["a_b_s_t"][slot], w["a_w_out"][slot], _row(post[1]))
        elif kind == 1:
            if rot is None:
                rot = _rotary_tables(seq)
            qkv = _qkv(x, _row(pre[1]), w["b_w_qkv"][slot], rot, seq)
            o = _diffattn(qkv, _row(w["b_lam_q1"][slot]), _row(w["b_lam_k1"][slot]), _row(w["b_lam_q2"][slot]),
                          _row(w["b_lam_k2"][slot]), _row(w["b_subln"][slot]), batch, seq, i)
            x = _proj_res(x, o, w["b_w_out"][slot], _row(post[1]))
        else:
            if pos_table is None:
                cp, sp = _dft_tables(seq)
                pos_table = jnp.concatenate([cp, -sp], axis=1).astype(BF16)
            ab = _fourier_chan(x, _row(pre[1]), w["chan_cs"], batch, seq)
            mixed = _fourier_pos(pos_table, ab, batch, seq)
            x = _proj_res(x, mixed, w["c_w_out"][slot], _row(post[1]))
        kv = _norm_mm(mem, _row(w["ln_mem"][i]), w["x_w_kv"][i])
        x = _xattn(x, _row(pre[2]), w["x_w_q"][i], kv, w["x_w_out"][i], _row(post[2]), seq)
        x = _ffn(x, _row(pre[3]), w["ffn_w_gate"][i, 1], w["ffn_w_up"][i, 1], w["ffn_w_down"][i, 1], _row(post[3]))
    return x.reshape(batch, seq, D)


def kernel(x_prompt, x_sample, mem_prompt, mem_sample, ln_pre, ln_post, ln_mem, ffn_w_gate, ffn_w_up, ffn_w_down, a_w_in, a_ln_g, a_ln_b, a_w_s, a_b_s, a_w_out, b_w_qkv, b_lam_q1, b_lam_k1, b_lam_q2, b_lam_k2, b_subln, b_w_out, c_w_out, x_w_q, x_w_kv, x_w_out):
    params = {
        "ln_pre": ln_pre, "ln_post": ln_post, "ln_mem": ln_mem,
        "ffn_w_gate": ffn_w_gate, "ffn_w_up": ffn_w_up, "ffn_w_down": ffn_w_down,
        "a_w_in": a_w_in, "a_ln_g": a_ln_g, "a_ln_b": a_ln_b, "a_w_s": a_w_s, "a_b_s": a_b_s, "a_w_out": a_w_out,
        "b_w_qkv": b_w_qkv, "b_lam_q1": b_lam_q1, "b_lam_k1": b_lam_k1, "b_lam_q2": b_lam_q2,
        "b_lam_k2": b_lam_k2, "b_subln": b_subln, "b_w_out": b_w_out,
        "c_w_out": c_w_out, "x_w_q": x_w_q, "x_w_kv": x_w_kv, "x_w_out": x_w_out,
    }
    depth = ln_pre.shape[0]
    w = _prepare(params)
    return (_trunk(x_prompt, mem_prompt, w, depth), _trunk(x_sample, mem_sample, w, depth))
```

```python
import functools
import math

import jax
import jax.numpy as jnp
from jax import lax
from jax.experimental import pallas as pl
from jax.experimental.pallas import tpu as pltpu

F32 = jnp.float32
BF16 = jnp.bfloat16

D = 2048
D_FF = 5504
LANE = 128
FF_PAD = 5632
CHUNK = 128
A_GROUPS = 8
A_GROUP_DIM = D // A_GROUPS
B_HEADS = 8
B_HEAD_DIM = 128
B_VALUE_DIM = 256
ROT_DIM = 32
ROPE_THETA = 500000.0
C_GROUPS = 8
C_GROUP_DIM = D // C_GROUPS
X_HEADS = 4
X_HEAD_DIM = 128
X_WIDTH = X_HEADS * X_HEAD_DIM
N_MEM = 256
NORM_EPS = 1e-6
SUBLN_EPS = 1e-5
LN_EPS = 1e-5
MIB = 1024 * 1024
VMEM_CAP_MIB = 60

FFN_TM, FFN_TF = 512, 512
ROW_TM = 512
ROW_CHUNK = 128
NORM_CHUNK = 256
ATT_TQ, ATT_TK = 256, 512
ATT_SCORE_MIB = 16
ATT_UNROLL = 8
Q_SCALE = B_HEAD_DIM ** -0.5 * math.log2(math.e)
DFT_TM, DFT_TK = 1024, 2048


def _params(semantics, vmem_mib):
    return pltpu.CompilerParams(dimension_semantics=semantics,
                                vmem_limit_bytes=min(vmem_mib, VMEM_CAP_MIB) * MIB)


def _row_chunks(n_rows, fn, rc=None):
    rc = min(rc or ROW_CHUNK, n_rows)

    def body(c, carry):
        fn(pl.ds(pl.multiple_of(c * rc, rc), rc))
        return carry

    lax.fori_loop(0, n_rows // rc, body, 0)


def _rms(x, g, eps):
    return x * lax.rsqrt(jnp.mean(x * x, axis=-1, keepdims=True) + eps) * g


def _norm_rows_to(h_ref, x_ref, g_ref):
    def fn(rs):
        h_ref[rs, :] = _rms(x_ref[rs, :], g_ref[...], NORM_EPS).astype(BF16)
    _row_chunks(x_ref.shape[0], fn, NORM_CHUNK)


def _postnorm_residual_to(o_ref, x_ref, y_ref, g_ref, scale):
    def fn(rs):
        o_ref[rs, :] = x_ref[rs, :] + scale * _rms(y_ref[rs, :], g_ref[...], NORM_EPS)
    _row_chunks(x_ref.shape[0], fn, NORM_CHUNK)


def _ffn_kernel(x_ref, gpre_ref, wg_ref, wu_ref, wd_ref, gpost_ref, o_ref, h_ref, acc_ref):
    f = pl.program_id(1)

    @pl.when(f == 0)
    def _():
        _norm_rows_to(h_ref, x_ref, gpre_ref)
        acc_ref[...] = jnp.zeros_like(acc_ref)

    h = h_ref[...]
    gate = jnp.dot(h, wg_ref[...], preferred_element_type=F32)
    up = jnp.dot(h, wu_ref[...], preferred_element_type=F32)
    a = (gate * jax.nn.sigmoid(gate) * up).astype(BF16)
    acc_ref[...] += jnp.dot(a, wd_ref[...], preferred_element_type=F32)

    @pl.when(f == pl.num_programs(1) - 1)
    def _():
        _postnorm_residual_to(o_ref, x_ref, acc_ref, gpost_ref, 0.5)


def _ffn(x, g_pre, wg, wu, wd, g_post):
    n = x.shape[0]
    tm = min(FFN_TM, n)
    tf = FFN_TF
    fp = wg.shape[1]
    vmem = (2 * 2 * tm * D * 4 + tm * D * (2 + 4) + 2 * 3 * D * tf * 2 + 3 * tm * tf * 4) // MIB + 6
    return pl.pallas_call(
        _ffn_kernel,
        grid=(n // tm, fp // tf),
        in_specs=[
            pl.BlockSpec((tm, D), lambda i, f: (i, 0)),
            pl.BlockSpec((1, D), lambda i, f: (0, 0)),
            pl.BlockSpec((D, tf), lambda i, f: (0, f)),
            pl.BlockSpec((D, tf), lambda i, f: (0, f)),
            pl.BlockSpec((tf, D), lambda i, f: (f, 0)),
            pl.BlockSpec((1, D), lambda i, f: (0, 0)),
        ],
        out_specs=pl.BlockSpec((tm, D), lambda i, f: (i, 0)),
        out_shape=jax.ShapeDtypeStruct((n, D), F32),
        scratch_shapes=[pltpu.VMEM((tm, D), BF16), pltpu.VMEM((tm, D), F32)],
        compiler_params=_params(("parallel", "arbitrary"), vmem),
        name="ffn",
    )(x, g_pre, wg, wu, wd, g_post)


def _norm_mm_kernel(x_ref, g_ref, w_ref, o_ref, h_ref):
    _norm_rows_to(h_ref, x_ref, g_ref)
    o_ref[...] = jnp.dot(h_ref[...], w_ref[...], preferred_element_type=F32).astype(o_ref.dtype)


def _norm_mm(x, g, w):
    n, n_out = x.shape[0], w.shape[1]
    tm = min(ROW_TM, n)
    tn = min(1024, n_out)
    vmem = (2 * tm * D * 4 + tm * D * 2 + 2 * D * tn * 2 + 2 * tm * tn * 2 + tm * tn * 4) // MIB + 6
    return pl.pallas_call(
        _norm_mm_kernel,
        grid=(n_out // tn, n // tm),
        in_specs=[
            pl.BlockSpec((tm, D), lambda j, i: (i, 0)),
            pl.BlockSpec((1, D), lambda j, i: (0, 0)),
            pl.BlockSpec((D, tn), lambda j, i: (0, j)),
        ],
        out_specs=pl.BlockSpec((tm, tn), lambda j, i: (i, j)),
        out_shape=jax.ShapeDtypeStruct((n, n_out), BF16),
        scratch_shapes=[pltpu.VMEM((tm, D), BF16)],
        compiler_params=_params(("parallel", "parallel"), vmem),
        name="norm_mm",
    )(x, g, w)


def _proj_res_kernel(x_ref, a_ref, w_ref, g_ref, o_ref, y_ref):
    y_ref[...] = jnp.dot(a_ref[...], w_ref[...], preferred_element_type=F32)
    _postnorm_residual_to(o_ref, x_ref, y_ref, g_ref, 1.0)


def _proj_res(x, a, w, g_post):
    n, k = a.shape
    tm = min(ROW_TM, n)
    vmem = (2 * 2 * tm * D * 4 + 2 * tm * k * 2 + 2 * k * D * 2 + tm * D * 4) // MIB + 6
    return pl.pallas_call(
        _proj_res_kernel,
        grid=(n // tm,),
        in_specs=[
            pl.BlockSpec((tm, D), lambda i: (i, 0)),
            pl.BlockSpec((tm, k), lambda i: (i, 0)),
            pl.BlockSpec((k, D), lambda i: (0, 0)),
            pl.BlockSpec((1, D), lambda i: (0, 0)),
        ],
        out_specs=pl.BlockSpec((tm, D), lambda i: (i, 0)),
        out_shape=jax.ShapeDtypeStruct((n, D), F32),
        scratch_shapes=[pltpu.VMEM((tm, D), F32)],
        compiler_params=_params(("parallel",), vmem),
        name="proj_res",
    )(x, a, w, g_post)


def _gelu(y):
    return 0.5 * y * (1.0 + lax.erf(y * (0.5 ** 0.5)))


def _gmlp_in_kernel(x_ref, g_ref, w_ref, lg_ref, lb_ref, o_ref, h_ref, y_ref):
    j = pl.program_id(0)
    _norm_rows_to(h_ref, x_ref, g_ref)
    y_ref[...] = jnp.dot(h_ref[...], w_ref[...], preferred_element_type=F32)

    @pl.when(j == 0)
    def _():
        def fn(rs):
            o_ref[rs, :] = _gelu(y_ref[rs, :]).astype(BF16)
        _row_chunks(x_ref.shape[0], fn)

    @pl.when(j == 1)
    def _():
        def fn(rs):
            v = _gelu(y_ref[rs, :])
            vc = v - jnp.mean(v, axis=-1, keepdims=True)
            var = jnp.mean(vc * vc, axis=-1, keepdims=True)
            o_ref[rs, :] = (vc * lax.rsqrt(var + LN_EPS) * lg_ref[...] + lb_ref[...]).astype(BF16)
        _row_chunks(x_ref.shape[0], fn)


def _gmlp_in(x, g_pre, w_in, ln_g, ln_b):
    n = x.shape[0]
    tm = min(ROW_TM, n)
    vmem = (2 * tm * D * 4 + tm * D * 2 + 2 * D * D * 2 + 2 * tm * D * 2 + tm * D * 4) // MIB + 6
    return pl.pallas_call(
        _gmlp_in_kernel,
        grid=(2, n // tm),
        in_specs=[
            pl.BlockSpec((tm, D), lambda j, i: (i, 0)),
            pl.BlockSpec((1, D), lambda j, i: (0, 0)),
            pl.BlockSpec((D, D), lambda j, i: (0, j)),
            pl.BlockSpec((1, D), lambda j, i: (0, 0)),
            pl.BlockSpec((1, D), lambda j, i: (0, 0)),
        ],
        out_specs=pl.BlockSpec((None, tm, D), lambda j, i: (j, i, 0)),
        out_shape=jax.ShapeDtypeStruct((2, n, D), BF16),
        scratch_shapes=[pltpu.VMEM((tm, D), BF16), pltpu.VMEM((tm, D), F32)],
        compiler_params=_params(("parallel", "parallel"), vmem),
        name="gmlp_in",
    )(x, g_pre, w_in, ln_g, ln_b)


def _gmlp_out_kernel(x_ref, u_ref, v_ref, ws_ref, bs_ref, w_ref, g_ref, o_ref, a_ref, y_ref):
    def fn(rs):
        for g in range(A_GROUPS):
            cs = slice(g * A_GROUP_DIM, (g + 1) * A_GROUP_DIM)
            mixed = jnp.dot(ws_ref[g], v_ref[rs, cs], preferred_element_type=F32) + bs_ref[:, g:g + 1]
            a_ref[rs, cs] = (u_ref[rs, cs].astype(F32) * mixed).astype(BF16)
    _row_chunks(x_ref.shape[0], fn)
    y_ref[...] = jnp.dot(a_ref[...], w_ref[...], preferred_element_type=F32)
    _postnorm_residual_to(o_ref, x_ref, y_ref, g_ref, 1.0)


def _gmlp_out(x, uv, w_s, b_s_t, w_out, g_post):
    n = x.shape[0]
    tm = min(ROW_TM, n)
    assert ROW_CHUNK == CHUNK and tm % CHUNK == 0
    vmem = (2 * 2 * tm * D * 4 + 2 * 2 * tm * D * 2 + 2 * D * D * 2 + tm * D * (2 + 4)) // MIB + 6
    return pl.pallas_call(
        _gmlp_out_kernel,
        grid=(n // tm,),
        in_specs=[
            pl.BlockSpec((tm, D), lambda i: (i, 0)),
            pl.BlockSpec((None, tm, D), lambda i: (0, i, 0)),
            pl.BlockSpec((None, tm, D), lambda i: (1, i, 0)),
            pl.BlockSpec((A_GROUPS, CHUNK, CHUNK), lambda i: (0, 0, 0)),
            pl.BlockSpec((CHUNK, A_GROUPS), lambda i: (0, 0)),
            pl.BlockSpec((D, D), lambda i: (0, 0)),
            pl.BlockSpec((1, D), lambda i: (0, 0)),
        ],
        out_specs=pl.BlockSpec((tm, D), lambda i: (i, 0)),
        out_shape=jax.ShapeDtypeStruct((n, D), F32),
        scratch_shapes=[pltpu.VMEM((tm, D), BF16), pltpu.VMEM((tm, D), F32)],
        compiler_params=_params(("parallel",), vmem),
        name="gmlp_out",
    )(x, uv, uv, w_s, b_s_t, w_out, g_post)


def _rotary_tables(s):
    half = ROT_DIM // 2
    inv_freq = 1.0 / (ROPE_THETA ** (jnp.arange(0, ROT_DIM, 2, dtype=F32) / ROT_DIM))
    ang = jnp.arange(s, dtype=F32)[:, None] * inv_freq[None, :]
    cos, sin = jnp.cos(ang), jnp.sin(ang)
    rest = B_HEAD_DIM - ROT_DIM
    ct = jnp.concatenate([cos, cos, jnp.ones((s, rest), F32)], axis=1)
    sa = jnp.concatenate([-sin, jnp.zeros((s, half + rest), F32)], axis=1)
    sb = jnp.concatenate([jnp.zeros((s, half), F32), sin, jnp.zeros((s, rest), F32)], axis=1)
    return ct, sa, sb


def _qkv_kernel(x_ref, g_ref, w_ref, ct_ref, sa_ref, sb_ref, o_ref, h_ref, y_ref):
    j = pl.program_id(0)
    _norm_rows_to(h_ref, x_ref, g_ref)
    y_ref[...] = jnp.dot(h_ref[...], w_ref[...], preferred_element_type=F32)

    @pl.when(j == 2)
    def _():
        def fn(rs):
            o_ref[rs, :] = y_ref[rs, :].astype(BF16)
        _row_chunks(x_ref.shape[0], fn)

    @pl.when(j < 2)
    def _():
        scale = jnp.where(j == 0, Q_SCALE, 1.0).astype(F32)

        def fn(rs):
            ct, sa, sb = ct_ref[rs, :], sa_ref[rs, :], sb_ref[rs, :]
            for c in range(D // LANE):
                cs = slice(c * LANE, (c + 1) * LANE)
                yc = y_ref[rs, cs]
                rot = (yc * ct + pltpu.roll(yc, LANE - ROT_DIM // 2, 1) * sa
                       + pltpu.roll(yc, ROT_DIM // 2, 1) * sb)
                o_ref[rs, cs] = (rot * scale).astype(BF16)
        _row_chunks(x_ref.shape[0], fn)


def _qkv(x, g_pre, w_qkv, tables, seq):
    n = x.shape[0]
    tm = min(ROW_TM, seq)
    pos_tiles = seq // tm
    ct, sa, sb = tables
    vmem = (2 * tm * D * 4 + tm * D * 2 + 2 * D * D * 2 + 2 * tm * D * 2 + tm * D * 4) // MIB + 8
    tab = pl.BlockSpec((tm, LANE), lambda j, i: (i % pos_tiles, 0))
    return pl.pallas_call(
        _qkv_kernel,
        grid=(3, n // tm),
        in_specs=[
            pl.BlockSpec((tm, D), lambda j, i: (i, 0)),
            pl.BlockSpec((1, D), lambda j, i: (0, 0)),
            pl.BlockSpec((D, D), lambda j, i: (0, j)),
            tab, tab, tab,
        ],
        out_specs=pl.BlockSpec((None, tm, D), lambda j, i: (j, i, 0)),
        out_shape=jax.ShapeDtypeStruct((3, n, D), BF16),
        scratch_shapes=[pltpu.VMEM((tm, D), BF16), pltpu.VMEM((tm, D), F32)],
        compiler_params=_params(("parallel", "parallel"), vmem),
        name="qkv",
    )(x, g_pre, w_qkv, ct, sa, sb)


def _diffattn_kernel(q_ref, k_ref, v_ref, lq1_ref, lk1_ref, lq2_ref, lk2_ref, sg_ref, o_ref,
                     s_ref, m_ref, l_ref, p_ref, acc_ref, *, tk, lam_init):
    seq = k_ref.shape[0]
    lane_blocks = tk // LANE
    unroll = max(1, min(ATT_UNROLL, seq // tk // 2))
    lam = (jnp.exp(jnp.sum(lq1_ref[...] * lk1_ref[...], axis=-1, keepdims=True))
           - jnp.exp(jnp.sum(lq2_ref[...] * lk2_ref[...], axis=-1, keepdims=True)) + lam_init)
    m_ref[...] = jnp.full_like(m_ref, -jnp.inf)
    l_ref[...] = jnp.zeros_like(l_ref)
    acc_ref[...] = jnp.zeros_like(acc_ref)

    def scores(t, carry):
        ks = pl.ds(pl.multiple_of(t * tk, tk), tk)
        for c in range(2):
            cs = slice(c * B_HEAD_DIM, (c + 1) * B_HEAD_DIM)
            s = lax.dot_general(q_ref[:, cs], k_ref[ks, cs], (((1,), (1,)), ((), ())),
                                preferred_element_type=F32)
            s_ref[c, t] = s
            m = m_ref[c]
            for j in range(lane_blocks):
                m = jnp.maximum(m, s[:, j * LANE:(j + 1) * LANE])
            m_ref[c] = m
        return carry

    lax.fori_loop(0, seq // tk, scores, 0, unroll=unroll)
    for c in range(2):
        m_ref[c] = jnp.broadcast_to(jnp.max(m_ref[c], axis=-1, keepdims=True), m_ref.shape[1:])

    def weights(t, carry):
        ks = pl.ds(pl.multiple_of(t * tk, tk), tk)
        for c in range(2):
            m = m_ref[c]
            l = l_ref[c]
            for j in range(lane_blocks):
                ls = slice(j * LANE, (j + 1) * LANE)
                p = jnp.exp2(s_ref[c, t, :, ls] - m)
                l = l + p
                p_ref[c, :, ls] = p.astype(BF16)
            l_ref[c] = l
            acc_ref[c] += jnp.dot(p_ref[c], v_ref[ks, :], preferred_element_type=F32)
        return carry

    lax.fori_loop(0, seq // tk, weights, 0, unroll=unroll)
    l0 = jnp.sum(l_ref[0], axis=-1, keepdims=True)
    l1 = jnp.sum(l_ref[1], axis=-1, keepdims=True)
    o = acc_ref[0] / l0 - lam * (acc_ref[1] / l1)
    o_ref[...] = (_rms(o, sg_ref[...], SUBLN_EPS) * (1.0 - lam_init)).astype(BF16)


def _diffattn(qkv, lam_q1, lam_k1, lam_q2, lam_k2, subln_g, batch, seq, layer_idx):
    n = batch * seq
    tq = min(ATT_TQ, seq, ATT_SCORE_MIB * MIB // (8 * seq))
    tk = min(ATT_TK, seq)
    q_tiles = seq // tq
    hw = 2 * B_HEAD_DIM
    lam_init = 0.8 - 0.6 * math.exp(-0.3 * layer_idx)
    vec = pl.BlockSpec((1, B_HEAD_DIM), lambda b, h, i: (0, 0))
    vmem = (2 * 2 * seq * hw * 2 + 4 * tq * hw * 2 + 2 * tq * seq * 4 + 2 * tq * hw * 4 + 4 * tq * tk * 4) // MIB + 8
    return pl.pallas_call(
        functools.partial(_diffattn_kernel, tk=tk, lam_init=lam_init),
        grid=(batch, B_HEADS, q_tiles),
        in_specs=[
            pl.BlockSpec((None, tq, hw), lambda b, h, i: (0, b * q_tiles + i, h)),
            pl.BlockSpec((None, seq, hw), lambda b, h, i: (1, b, h)),
            pl.BlockSpec((None, seq, hw), lambda b, h, i: (2, b, h)),
            vec, vec, vec, vec,
            pl.BlockSpec((1, B_VALUE_DIM), lambda b, h, i: (0, 0)),
        ],
        out_specs=pl.BlockSpec((tq, hw), lambda b, h, i: (b * q_tiles + i, h)),
        out_shape=jax.ShapeDtypeStruct((n, D), BF16),
        scratch_shapes=[pltpu.VMEM((2, seq // tk, tq, tk), F32),
                        pltpu.VMEM((2, tq, LANE), F32), pltpu.VMEM((2, tq, LANE), F32),
                        pltpu.VMEM((2, tq, tk), BF16), pltpu.VMEM((2, tq, B_VALUE_DIM), F32)],
        compiler_params=_params(("parallel", "parallel", "arbitrary"), vmem),
        name="diffattn",
    )(qkv, qkv, qkv, lam_q1, lam_k1, lam_q2, lam_k2, subln_g)


def _dft_tables(n):
    idx = jnp.arange(n, dtype=jnp.int32)
    ang = ((idx[:, None] * idx[None, :]) % n).astype(F32) * (2.0 * math.pi / n)
    return jnp.cos(ang), jnp.sin(ang)


def _fourier_chan_kernel(x_ref, g_ref, cs_ref, o_ref, h_ref):
    _norm_rows_to(h_ref, x_ref, g_ref)
    for g in range(C_GROUPS):
        cs = slice(g * C_GROUP_DIM, (g + 1) * C_GROUP_DIM)
        ab = jnp.dot(h_ref[:, cs], cs_ref[...], preferred_element_type=F32)
        o_ref[0, :, cs] = ab[:, :C_GROUP_DIM].astype(BF16)
        o_ref[1, :, cs] = ab[:, C_GROUP_DIM:].astype(BF16)


def _fourier_chan(x, g_pre, chan_cs, batch, seq):
    tm = min(ROW_TM, seq)
    tiles = seq // tm
    vmem = (2 * tm * D * 4 + tm * D * 2 + 2 * 2 * tm * D * 2) // MIB + 8
    return pl.pallas_call(
        _fourier_chan_kernel,
        grid=(batch, tiles),
        in_specs=[
            pl.BlockSpec((tm, D), lambda b, i: (b * tiles + i, 0)),
            pl.BlockSpec((1, D), lambda b, i: (0, 0)),
            pl.BlockSpec((C_GROUP_DIM, 2 * C_GROUP_DIM), lambda b, i: (0, 0)),
        ],
        out_specs=pl.BlockSpec((None, 2, tm, D), lambda b, i: (b, 0, i, 0)),
        out_shape=jax.ShapeDtypeStruct((batch, 2, seq, D), BF16),
        scratch_shapes=[pltpu.VMEM((tm, D), BF16)],
        compiler_params=_params(("parallel", "parallel"), vmem),
        name="fourier_chan",
    )(x, g_pre, chan_cs)


def _pos_factor_tables(seq):
    k = jnp.arange(seq, dtype=jnp.int32)

    def cos_sin(j):
        ang = ((j[:, None] * k[None, :]) % seq).astype(F32) * (2.0 * math.pi / seq)
        return jnp.cos(ang), jnp.sin(ang)

    ca, sa = cos_sin(jnp.arange(seq // LANE, dtype=jnp.int32) * LANE)
    cb, sb = cos_sin(jnp.arange(LANE, dtype=jnp.int32))
    return (jnp.concatenate([ca, -sa], axis=1), jnp.concatenate([-sa, -ca], axis=1),
            jnp.concatenate([cb, cb], axis=1), jnp.concatenate([sb, sb], axis=1))


def _fourier_pos_kernel(p_ref, q_ref, u_ref, v_ref, ab_ref, o_ref, t_ref, acc_ref, *, norm):
    k = pl.program_id(2)

    @pl.when(k == 0)
    def _():
        acc_ref[...] = jnp.zeros_like(acc_ref)

    for g in range(t_ref.shape[0] // LANE):
        t_ref[g * LANE:(g + 1) * LANE, :] = (p_ref[g:g + 1, :] * u_ref[...]
                                             + q_ref[g:g + 1, :] * v_ref[...]).astype(BF16)
    acc_ref[...] += jnp.dot(t_ref[...], ab_ref[...], preferred_element_type=F32)

    @pl.when(k == pl.num_programs(2) - 1)
    def _():
        def fn(rs):
            o_ref[rs, :] = (acc_ref[rs, :] * norm).astype(BF16)
        _row_chunks(acc_ref.shape[0], fn)


def _fourier_pos(tables, ab, batch, seq):
    tm = min(DFT_TM, seq)
    tk = min(DFT_TK, 2 * seq)
    tiles = seq // tm
    ab2 = ab.reshape(batch * 2 * seq, D)
    k_tiles = 2 * seq // tk
    groups = tm // LANE
    vmem = (tm * tk * 2 + 4 * LANE * tk * 4 + 2 * tk * D * 2 + tm * D * 4 + 2 * tm * D * 2) // MIB + 8
    coarse = pl.BlockSpec((groups, tk), lambda b, i, k: (i, k))
    fine = pl.BlockSpec((LANE, tk), lambda b, i, k: (0, k))
    return pl.pallas_call(
        functools.partial(_fourier_pos_kernel, norm=(seq * C_GROUP_DIM) ** -0.5),
        grid=(batch, tiles, k_tiles),
        in_specs=[coarse, coarse, fine, fine,
                  pl.BlockSpec((tk, D), lambda b, i, k: (b * k_tiles + k, 0))],
        out_specs=pl.BlockSpec((tm, D), lambda b, i, k: (b * tiles + i, 0)),
        out_shape=jax.ShapeDtypeStruct((batch * seq, D), BF16),
        scratch_shapes=[pltpu.VMEM((tm, tk), BF16), pltpu.VMEM((tm, D), F32)],
        compiler_params=_params(("parallel", "parallel", "arbitrary"), vmem),
        name="fourier_pos",
    )(*tables, ab2)


def _xattn_kernel(x_ref, gpre_ref, wq_ref, kv_ref, wo_ref, gpost_ref, o_ref, h_ref, q_ref, a_ref, y_ref):
    _norm_rows_to(h_ref, x_ref, gpre_ref)
    q_ref[...] = (jnp.dot(h_ref[...], wq_ref[...], preferred_element_type=F32)
                  * (X_HEAD_DIM ** -0.5)).astype(BF16)

    for h in range(X_HEADS):
        cs = slice(h * X_HEAD_DIM, (h + 1) * X_HEAD_DIM)
        vs = slice(X_WIDTH + h * X_HEAD_DIM, X_WIDTH + (h + 1) * X_HEAD_DIM)
        s = lax.dot_general(q_ref[:, cs], kv_ref[:, cs], (((1,), (1,)), ((), ())),
                            preferred_element_type=F32)
        e = jnp.exp(s - jnp.max(s, axis=-1, keepdims=True))
        pv = jnp.dot(e.astype(BF16), kv_ref[:, vs], preferred_element_type=F32)
        a_ref[:, cs] = (pv / jnp.sum(e, axis=-1, keepdims=True)).astype(BF16)
    y_ref[...] = jnp.dot(a_ref[...], wo_ref[...], preferred_element_type=F32)
    _postnorm_residual_to(o_ref, x_ref, y_ref, gpost_ref, 1.0)


def _xattn(x, g_pre, w_q, kv, w_out, g_post, seq):
    n = x.shape[0]
    tm = min(ROW_TM, seq)
    tiles = seq // tm
    vmem = (2 * 2 * tm * D * 4 + tm * D * (2 + 4) + 4 * D * X_WIDTH * 2 + 2 * tm * X_WIDTH * 2) // MIB + 8
    return pl.pallas_call(
        _xattn_kernel,
        grid=(n // tm,),
        in_specs=[
            pl.BlockSpec((tm, D), lambda i: (i, 0)),
            pl.BlockSpec((1, D), lambda i: (0, 0)),
            pl.BlockSpec((D, X_WIDTH), lambda i: (0, 0)),
            pl.BlockSpec((N_MEM, 2 * X_WIDTH), lambda i: (i // tiles, 0)),
            pl.BlockSpec((X_WIDTH, D), lambda i: (0, 0)),
            pl.BlockSpec((1, D), lambda i: (0, 0)),
        ],
        out_specs=pl.BlockSpec((tm, D), lambda i: (i, 0)),
        out_shape=jax.ShapeDtypeStruct((n, D), F32),
        scratch_shapes=[pltpu.VMEM((tm, D), BF16), pltpu.VMEM((tm, X_WIDTH), BF16),
                        pltpu.VMEM((tm, X_WIDTH), BF16), pltpu.VMEM((tm, D), F32)],
        compiler_params=_params(("parallel",), vmem),
        name="xattn",
    )(x, g_pre, w_q, kv, w_out, g_post)


def _prepare(p):
    pad = FF_PAD - D_FF
    w = dict(p)
    w["ffn_w_gate"] = jnp.pad(p["ffn_w_gate"].astype(BF16), ((0, 0), (0, 0), (0, 0), (0, pad)))
    w["ffn_w_up"] = jnp.pad(p["ffn_w_up"].astype(BF16), ((0, 0), (0, 0), (0, 0), (0, pad)))
    w["ffn_w_down"] = jnp.pad(p["ffn_w_down"].astype(BF16), ((0, 0), (0, 0), (0, pad), (0, 0)))
    for name in ("a_w_in", "a_w_s", "a_w_out", "b_w_qkv", "b_w_out", "c_w_out", "x_w_q", "x_w_kv", "x_w_out"):
        w[name] = p[name].astype(BF16)
    w["a_b_s_t"] = jnp.swapaxes(p["a_b_s"], 1, 2)
    cc, sc = _dft_tables(C_GROUP_DIM)
    w["chan_cs"] = jnp.concatenate([cc, sc], axis=1).astype(BF16)
    return w


def _row(v):
    return v.reshape(1, -1)


def _trunk(x3, mem3, w, depth):
    batch, seq, _ = x3.shape
    x = x3.reshape(batch * seq, D)
    mem = mem3.reshape(batch * N_MEM, D)
    n_mixers = 3
    rot = pos_table = None
    for i in range(depth):
        kind, slot = i % n_mixers, i // n_mixers
        pre, post = w["ln_pre"][i], w["ln_post"][i]
        x = _ffn(x, _row(pre[0]), w["ffn_w_gate"][i, 0], w["ffn_w_up"][i, 0], w["ffn_w_down"][i, 0], _row(post[0]))
        if kind == 0:
            uv = _gmlp_in(x, _row(pre[1]), w["a_w_in"][slot], _row(w["a_ln_g"][slot]), _row(w["a_ln_b"][slot]))
            x = _gmlp_out(x, uv, w["a_w_s"][slot], w["a_b_s_t"][slot], w["a_w_out"][slot], _row(post[1]))
        elif kind == 1:
            if rot is None:
                rot = _rotary_tables(seq)
            qkv = _qkv(x, _row(pre[1]), w["b_w_qkv"][slot], rot, seq)
            o = _diffattn(qkv, _row(w["b_lam_q1"][slot]), _row(w["b_lam_k1"][slot]), _row(w["b_lam_q2"][slot]),
                          _row(w["b_lam_k2"][slot]), _row(w["b_subln"][slot]), batch, seq, i)
            x = _proj_res(x, o, w["b_w_out"][slot], _row(post[1]))
        else:
            if pos_table is None:
                pos_table = _pos_factor_tables(seq)
            ab = _fourier_chan(x, _row(pre[1]), w["chan_cs"], batch, seq)
            mixed = _fourier_pos(pos_table, ab, batch, seq)
            x = _proj_res(x, mixed, w["c_w_out"][slot], _row(post[1]))
        kv = _norm_mm(mem, _row(w["ln_mem"][i]), w["x_w_kv"][i])
        x = _xattn(x, _row(pre[2]), w["x_w_q"][i], kv, w["x_w_out"][i], _row(post[2]), seq)
        x = _ffn(x, _row(pre[3]), w["ffn_w_gate"][i, 1], w["ffn_w_up"][i, 1], w["ffn_w_down"][i, 1], _row(post[3]))
    return x.reshape(batch, seq, D)


def kernel(x_prompt, x_sample, mem_prompt, mem_sample, ln_pre, ln_post, ln_mem, ffn_w_gate, ffn_w_up, ffn_w_down, a_w_in, a_ln_g, a_ln_b, a_w_s, a_b_s, a_w_out, b_w_qkv, b_lam_q1, b_lam_k1, b_lam_q2, b_lam_k2, b_subln, b_w_out, c_w_out, x_w_q, x_w_kv, x_w_out):
    params = {
        "ln_pre": ln_pre, "ln_post": ln_post, "ln_mem": ln_mem,
        "ffn_w_gate": ffn_w_gate, "ffn_w_up": ffn_w_up, "ffn_w_down": ffn_w_down,
        "a_w_in": a_w_in, "a_ln_g": a_ln_g, "a_ln_b": a_ln_b, "a_w_s": a_w_s, "a_b_s": a_b_s, "a_w_out": a_w_out,
        "b_w_qkv": b_w_qkv, "b_lam_q1": b_lam_q1, "b_lam_k1": b_lam_k1, "b_lam_q2": b_lam_q2,
        "b_lam_k2": b_lam_k2, "b_subln": b_subln, "b_w_out": b_w_out,
        "c_w_out": c_w_out, "x_w_q": x_w_q, "x_w_kv": x_w_kv, "x_w_out": x_w_out,
    }
    depth = ln_pre.shape[0]
    w = _prepare(params)
    return (_trunk(x_prompt, mem_prompt, w, depth), _trunk(x_sample, mem_sample, w, depth))
```

```python
import functools
import math

import jax
import jax.numpy as jnp
from jax import lax
from jax.experimental import pallas as pl
from jax.experimental.pallas import tpu as pltpu

F32 = jnp.float32
BF16 = jnp.bfloat16

D = 2048
D_FF = 5504
LANE = 128
FF_PAD = 5632
CHUNK = 128
A_GROUPS = 8
A_GROUP_DIM = D // A_GROUPS
B_HEADS = 8
B_HEAD_DIM = 128
B_VALUE_DIM = 256
ROT_DIM = 32
ROPE_THETA = 500000.0
C_GROUPS = 8
C_GROUP_DIM = D // C_GROUPS
X_HEADS = 4
X_HEAD_DIM = 128
X_WIDTH = X_HEADS * X_HEAD_DIM
N_MEM = 256
NORM_EPS = 1e-6
SUBLN_EPS = 1e-5
LN_EPS = 1e-5
MIB = 1024 * 1024
VMEM_CAP_MIB = 60

FFN_TM, FFN_TF = 1024, 512
FFN_PIECES = 8
ROW_TM = 512
ROW_CHUNK = 128
NORM_CHUNK = 256
COL_BLOCK = 512
ATT_TQ, ATT_TK = 256, 512
ATT_SCORE_MIB = 16
ATT_UNROLL = 8
Q_SCALE = B_HEAD_DIM ** -0.5 * math.log2(math.e)
DFT_TM, DFT_TK = 1024, 2048


def _params(semantics, vmem_mib, flags=None):
    return pltpu.CompilerParams(dimension_semantics=semantics, flags=flags,
                                vmem_limit_bytes=min(vmem_mib, VMEM_CAP_MIB) * MIB)


def _row_chunks(n_rows, fn, rc=None):
    rc = min(rc or ROW_CHUNK, n_rows)

    def body(c, carry):
        fn(pl.ds(pl.multiple_of(c * rc, rc), rc))
        return carry

    lax.fori_loop(0, n_rows // rc, body, 0)


def _rms(x, g, eps):
    return x * lax.rsqrt(jnp.mean(x * x, axis=-1, keepdims=True) + eps) * g


def _norm_rows_to(h_ref, x_ref, g_ref):
    def fn(rs):
        h_ref[rs, :] = _rms(x_ref[rs, :], g_ref[...], NORM_EPS).astype(BF16)
    _row_chunks(x_ref.shape[0], fn, NORM_CHUNK)


def _postnorm_residual_to(o_ref, x_ref, y_ref, g_ref, scale):
    def fn(rs):
        o_ref[rs, :] = x_ref[rs, :] + scale * _rms(y_ref[rs, :], g_ref[...], NORM_EPS)
    _row_chunks(x_ref.shape[0], fn, NORM_CHUNK)


def _ffn_kernel(xn_ref, xp_ref, gpre_ref, wg_ref, wu_ref, wd_ref, gpost_ref, o_ref, h_ref, acc_ref,
                *, n_tiles, pieces):
    i = pl.program_id(0)
    f = pl.program_id(1)
    side = i % 2
    main = 1 - side
    pr = xn_ref.shape[0]
    rows = pl.ds(pl.multiple_of(jnp.minimum(f, pieces - 1) * pr, pr), pr)

    def norm_in():
        h_ref[side, rows, :] = _rms(xn_ref[...], gpre_ref[...], NORM_EPS).astype(BF16)

    def finish():
        o_ref[...] = xp_ref[...] + 0.5 * _rms(acc_ref[side, rows, :], gpost_ref[...], NORM_EPS)

    def step(first):
        h = h_ref[main]
        gate = jnp.dot(h, wg_ref[...], preferred_element_type=F32)
        up = jnp.dot(h, wu_ref[...], preferred_element_type=F32)
        a = (gate * jax.nn.sigmoid(gate) * up).astype(BF16)
        d = jnp.dot(a, wd_ref[...], preferred_element_type=F32)
        if first:
            acc_ref[main] = d
        else:
            acc_ref[main] += d
        norm_in()
        finish()

    @pl.when(i == 0)
    def _():
        norm_in()

        @pl.when(f == 0)
        def _():
            acc_ref[...] = jnp.zeros_like(acc_ref)

    in_main = jnp.logical_and(i >= 1, i <= n_tiles)

    @pl.when(jnp.logical_and(in_main, f == 0))
    def _():
        step(True)

    @pl.when(jnp.logical_and(in_main, f > 0))
    def _():
        step(False)

    @pl.when(i == n_tiles + 1)
    def _():
        finish()


def _ffn(x, g_pre, wg, wu, wd, g_post):
    n = x.shape[0]
    tm = min(FFN_TM, n)
    tf = FFN_TF
    fp = wg.shape[1]
    n_tiles, nf = n // tm, fp // tf
    pieces = FFN_PIECES
    pr = tm // pieces
    assert pieces <= nf and pr % 16 == 0

    def piece(tile, f):
        return (jnp.where(tile < 0, 0, jnp.minimum(tile, n_tiles - 1) * pieces + jnp.minimum(f, pieces - 1)), 0)

    def wcol(i, f):
        return jnp.where(jnp.logical_and(i >= 1, i <= n_tiles), f, 0)

    vmem = (3 * 2 * pr * D * 4 + 2 * tm * D * (2 + 4) + 2 * 3 * D * tf * 2 + 4 * tm * tf * 4) // MIB + 6
    return pl.pallas_call(
        functools.partial(_ffn_kernel, n_tiles=n_tiles, pieces=pieces),
        grid=(n_tiles + 2, nf),
        in_specs=[
            pl.BlockSpec((pr, D), lambda i, f: piece(i, f)),
            pl.BlockSpec((pr, D), lambda i, f: piece(i - 2, f)),
            pl.BlockSpec((1, D), lambda i, f: (0, 0)),
            pl.BlockSpec((D, tf), lambda i, f: (0, wcol(i, f))),
            pl.BlockSpec((D, tf), lambda i, f: (0, wcol(i, f))),
            pl.BlockSpec((tf, D), lambda i, f: (wcol(i, f), 0)),
            pl.BlockSpec((1, D), lambda i, f: (0, 0)),
        ],
        out_specs=pl.BlockSpec((pr, D), lambda i, f: piece(i - 2, f)),
        out_shape=jax.ShapeDtypeStruct((n, D), F32),
        scratch_shapes=[pltpu.VMEM((2, tm, D), BF16), pltpu.VMEM((2, tm, D), F32)],
        compiler_params=_params(("arbitrary", "arbitrary"), vmem),
        name="ffn",
    )(x, x, g_pre, wg, wu, wd, g_post)


def _norm_mm_kernel(x_ref, g_ref, w_ref, o_ref, h_ref):
    _norm_rows_to(h_ref, x_ref, g_ref)
    o_ref[...] = jnp.dot(h_ref[...], w_ref[...], preferred_element_type=F32).astype(o_ref.dtype)


def _norm_mm(x, g, w):
    n, n_out = x.shape[0], w.shape[1]
    tm = min(ROW_TM, n)
    tn = min(1024, n_out)
    vmem = (2 * tm * D * 4 + tm * D * 2 + 2 * D * tn * 2 + 2 * tm * tn * 2 + tm * tn * 4) // MIB + 6
    return pl.pallas_call(
        _norm_mm_kernel,
        grid=(n_out // tn, n // tm),
        in_specs=[
            pl.BlockSpec((tm, D), lambda j, i: (i, 0)),
            pl.BlockSpec((1, D), lambda j, i: (0, 0)),
            pl.BlockSpec((D, tn), lambda j, i: (0, j)),
        ],
        out_specs=pl.BlockSpec((tm, tn), lambda j, i: (i, j)),
        out_shape=jax.ShapeDtypeStruct((n, n_out), BF16),
        scratch_shapes=[pltpu.VMEM((tm, D), BF16)],
        compiler_params=_params(("parallel", "parallel"), vmem),
        name="norm_mm",
    )(x, g, w)


def _proj_res_kernel(x_ref, a_ref, w_ref, g_ref, o_ref, y_ref):
    y_ref[...] = jnp.dot(a_ref[...], w_ref[...], preferred_element_type=F32)
    _postnorm_residual_to(o_ref, x_ref, y_ref, g_ref, 1.0)


def _proj_res(x, a, w, g_post):
    n, k = a.shape
    tm = min(ROW_TM, n)
    vmem = (2 * 2 * tm * D * 4 + 2 * tm * k * 2 + 2 * k * D * 2 + tm * D * 4) // MIB + 6
    return pl.pallas_call(
        _proj_res_kernel,
        grid=(n // tm,),
        in_specs=[
            pl.BlockSpec((tm, D), lambda i: (i, 0)),
            pl.BlockSpec((tm, k), lambda i: (i, 0)),
            pl.BlockSpec((k, D), lambda i: (0, 0)),
            pl.BlockSpec((1, D), lambda i: (0, 0)),
        ],
        out_specs=pl.BlockSpec((tm, D), lambda i: (i, 0)),
        out_shape=jax.ShapeDtypeStruct((n, D), F32),
        scratch_shapes=[pltpu.VMEM((tm, D), F32)],
        compiler_params=_params(("parallel",), vmem),
        name="proj_res",
    )(x, a, w, g_post)


def _gelu(y):
    return 0.5 * y * (1.0 + lax.erf(y * (0.5 ** 0.5)))


def _gmlp_in_kernel(x_ref, g_ref, w_ref, lg_ref, lb_ref, o_ref, h_ref, y_ref):
    j = pl.program_id(0)
    _norm_rows_to(h_ref, x_ref, g_ref)

    def gelu_block(c):
        cs = slice(c * COL_BLOCK, (c + 1) * COL_BLOCK)
        return cs, _gelu(jnp.dot(h_ref[...], w_ref[:, cs], preferred_element_type=F32))

    @pl.when(j == 0)
    def _():
        for c in range(D // COL_BLOCK):
            cs, y = gelu_block(c)
            o_ref[:, cs] = y.astype(BF16)

    @pl.when(j == 1)
    def _():
        for c in range(D // COL_BLOCK):
            cs, y = gelu_block(c)
            y_ref[:, cs] = y

        def fn(rs):
            v = y_ref[rs, :]
            vc = v - jnp.mean(v, axis=-1, keepdims=True)
            var = jnp.mean(vc * vc, axis=-1, keepdims=True)
            o_ref[rs, :] = (vc * lax.rsqrt(var + LN_EPS) * lg_ref[...] + lb_ref[...]).astype(BF16)
        _row_chunks(x_ref.shape[0], fn)


def _gmlp_in(x, g_pre, w_in, ln_g, ln_b):
    n = x.shape[0]
    tm = min(ROW_TM, n)
    vmem = (2 * tm * D * 4 + tm * D * 2 + 2 * D * D * 2 + 2 * tm * D * 2 + tm * D * 4) // MIB + 6
    return pl.pallas_call(
        _gmlp_in_kernel,
        grid=(2, n // tm),
        in_specs=[
            pl.BlockSpec((tm, D), lambda j, i: (i, 0)),
            pl.BlockSpec((1, D), lambda j, i: (0, 0)),
            pl.BlockSpec((D, D), lambda j, i: (0, j)),
            pl.BlockSpec((1, D), lambda j, i: (0, 0)),
            pl.BlockSpec((1, D), lambda j, i: (0, 0)),
        ],
        out_specs=pl.BlockSpec((None, tm, D), lambda j, i: (j, i, 0)),
        out_shape=jax.ShapeDtypeStruct((2, n, D), BF16),
        scratch_shapes=[pltpu.VMEM((tm, D), BF16), pltpu.VMEM((tm, D), F32)],
        compiler_params=_params(("parallel", "parallel"), vmem),
        name="gmlp_in",
    )(x, g_pre, w_in, ln_g, ln_b)


def _gmlp_out_kernel(x_ref, u_ref, v_ref, ws_ref, bs_ref, w_ref, g_ref, o_ref, a_ref, y_ref):
    def fn(rs):
        for g in range(A_GROUPS):
            cs = slice(g * A_GROUP_DIM, (g + 1) * A_GROUP_DIM)
            mixed = jnp.dot(ws_ref[g], v_ref[rs, cs], preferred_element_type=F32) + bs_ref[:, g:g + 1]
            a_ref[rs, cs] = (u_ref[rs, cs].astype(F32) * mixed).astype(BF16)
    _row_chunks(x_ref.shape[0], fn)
    y_ref[...] = jnp.dot(a_ref[...], w_ref[...], preferred_element_type=F32)
    _postnorm_residual_to(o_ref, x_ref, y_ref, g_ref, 1.0)


def _gmlp_out(x, uv, w_s, b_s_t, w_out, g_post):
    n = x.shape[0]
    tm = min(ROW_TM, n)
    assert ROW_CHUNK == CHUNK and tm % CHUNK == 0
    vmem = (2 * 2 * tm * D * 4 + 2 * 2 * tm * D * 2 + 2 * D * D * 2 + tm * D * (2 + 4)) // MIB + 6
    return pl.pallas_call(
        _gmlp_out_kernel,
        grid=(n // tm,),
        in_specs=[
            pl.BlockSpec((tm, D), lambda i: (i, 0)),
            pl.BlockSpec((None, tm, D), lambda i: (0, i, 0)),
            pl.BlockSpec((None, tm, D), lambda i: (1, i, 0)),
            pl.BlockSpec((A_GROUPS, CHUNK, CHUNK), lambda i: (0, 0, 0)),
            pl.BlockSpec((CHUNK, A_GROUPS), lambda i: (0, 0)),
            pl.BlockSpec((D, D), lambda i: (0, 0)),
            pl.BlockSpec((1, D), lambda i: (0, 0)),
        ],
        out_specs=pl.BlockSpec((tm, D), lambda i: (i, 0)),
        out_shape=jax.ShapeDtypeStruct((n, D), F32),
        scratch_shapes=[pltpu.VMEM((tm, D), BF16), pltpu.VMEM((tm, D), F32)],
        compiler_params=_params(("parallel",), vmem),
        name="gmlp_out",
    )(x, uv, uv, w_s, b_s_t, w_out, g_post)


def _rotary_tables(s):
    half = ROT_DIM // 2
    inv_freq = 1.0 / (ROPE_THETA ** (jnp.arange(0, ROT_DIM, 2, dtype=F32) / ROT_DIM))
    ang = jnp.arange(s, dtype=F32)[:, None] * inv_freq[None, :]
    cos, sin = jnp.cos(ang), jnp.sin(ang)
    rest = B_HEAD_DIM - ROT_DIM
    ct = jnp.concatenate([cos, cos, jnp.ones((s, rest), F32)], axis=1)
    sa = jnp.concatenate([-sin, jnp.zeros((s, half + rest), F32)], axis=1)
    sb = jnp.concatenate([jnp.zeros((s, half), F32), sin, jnp.zeros((s, rest), F32)], axis=1)
    return ct, sa, sb


def _qkv_kernel(x_ref, g_ref, w_ref, ct_ref, sa_ref, sb_ref, o_ref, h_ref, t_ref):
    j = pl.program_id(0)
    _norm_rows_to(h_ref, x_ref, g_ref)

    def block(c):
        return jnp.dot(h_ref[...], w_ref[:, c * COL_BLOCK:(c + 1) * COL_BLOCK], preferred_element_type=F32)

    @pl.when(j == 2)
    def _():
        for c in range(D // COL_BLOCK):
            o_ref[:, c * COL_BLOCK:(c + 1) * COL_BLOCK] = block(c).astype(BF16)

    @pl.when(j < 2)
    def _():
        scale = jnp.where(j == 0, Q_SCALE, 1.0).astype(F32)
        t_ref[0] = ct_ref[...] * scale
        t_ref[1] = sa_ref[...] * scale
        t_ref[2] = sb_ref[...] * scale
        for c in range(D // COL_BLOCK):
            y = block(c)
            for k in range(COL_BLOCK // LANE):
                yc = y[:, k * LANE:(k + 1) * LANE]
                rot = (yc * t_ref[0] + pltpu.roll(yc, LANE - ROT_DIM // 2, 1) * t_ref[1]
                       + pltpu.roll(yc, ROT_DIM // 2, 1) * t_ref[2])
                lo = c * COL_BLOCK + k * LANE
                o_ref[:, lo:lo + LANE] = rot.astype(BF16)


def _qkv(x, g_pre, w_qkv, tables, seq):
    n = x.shape[0]
    tm = min(ROW_TM, seq)
    pos_tiles = seq // tm
    ct, sa, sb = tables
    vmem = (2 * tm * D * 4 + tm * D * 2 + 2 * D * D * 2 + 2 * tm * D * 2 + tm * D * 4) // MIB + 8
    tab = pl.BlockSpec((tm, LANE), lambda j, i: (i % pos_tiles, 0))
    return pl.pallas_call(
        _qkv_kernel,
        grid=(3, n // tm),
        in_specs=[
            pl.BlockSpec((tm, D), lambda j, i: (i, 0)),
            pl.BlockSpec((1, D), lambda j, i: (0, 0)),
            pl.BlockSpec((D, D), lambda j, i: (0, j)),
            tab, tab, tab,
        ],
        out_specs=pl.BlockSpec((None, tm, D), lambda j, i: (j, i, 0)),
        out_shape=jax.ShapeDtypeStruct((3, n, D), BF16),
        scratch_shapes=[pltpu.VMEM((tm, D), BF16), pltpu.VMEM((3, tm, LANE), F32)],
        compiler_params=_params(("parallel", "parallel"), vmem),
        name="qkv",
    )(x, g_pre, w_qkv, ct, sa, sb)


def _diffattn_kernel(q_ref, k_ref, v_ref, lq1_ref, lk1_ref, lq2_ref, lk2_ref, sg_ref, o_ref,
                     s_ref, m_ref, l_ref, p_ref, acc_ref, *, tk, lam_init):
    seq = k_ref.shape[0]
    lane_blocks = tk // LANE
    unroll = max(1, min(ATT_UNROLL, seq // tk // 2))
    lam = (jnp.exp(jnp.sum(lq1_ref[...] * lk1_ref[...], axis=-1, keepdims=True))
           - jnp.exp(jnp.sum(lq2_ref[...] * lk2_ref[...], axis=-1, keepdims=True)) + lam_init)
    m_ref[...] = jnp.full_like(m_ref, -jnp.inf)
    l_ref[...] = jnp.zeros_like(l_ref)
    acc_ref[...] = jnp.zeros_like(acc_ref)

    def scores(t, carry):
        ks = pl.ds(pl.multiple_of(t * tk, tk), tk)
        for c in range(2):
            cs = slice(c * B_HEAD_DIM, (c + 1) * B_HEAD_DIM)
            s = lax.dot_general(q_ref[:, cs], k_ref[ks, cs], (((1,), (1,)), ((), ())),
                                preferred_element_type=F32)
            s_ref[c, t] = s
            m = m_ref[c]
            for j in range(lane_blocks):
                m = jnp.maximum(m, s[:, j * LANE:(j + 1) * LANE])
            m_ref[c] = m
        return carry

    lax.fori_loop(0, seq // tk, scores, 0, unroll=unroll)
    for c in range(2):
        m_ref[c] = jnp.broadcast_to(jnp.max(m_ref[c], axis=-1, keepdims=True), m_ref.shape[1:])

    def weights(t, carry):
        ks = pl.ds(pl.multiple_of(t * tk, tk), tk)
        for c in range(2):
            m = m_ref[c]
            l = l_ref[c]
            for j in range(lane_blocks):
                ls = slice(j * LANE, (j + 1) * LANE)
                p = jnp.exp2(s_ref[c, t, :, ls] - m)
                l = l + p
                p_ref[c, :, ls] = p.astype(BF16)
            l_ref[c] = l
            acc_ref[c] += jnp.dot(p_ref[c], v_ref[ks, :], preferred_element_type=F32)
        return carry

    lax.fori_loop(0, seq // tk, weights, 0, unroll=unroll)
    l0 = jnp.sum(l_ref[0], axis=-1, keepdims=True)
    l1 = jnp.sum(l_ref[1], axis=-1, keepdims=True)
    o = acc_ref[0] / l0 - lam * (acc_ref[1] / l1)
    o_ref[...] = (_rms(o, sg_ref[...], SUBLN_EPS) * (1.0 - lam_init)).astype(BF16)


def _diffattn(qkv, lam_q1, lam_k1, lam_q2, lam_k2, subln_g, batch, seq, layer_idx):
    n = batch * seq
    tq = min(ATT_TQ, seq, ATT_SCORE_MIB * MIB // (8 * seq))
    tk = min(ATT_TK, seq)
    q_tiles = seq // tq
    hw = 2 * B_HEAD_DIM
    lam_init = 0.8 - 0.6 * math.exp(-0.3 * layer_idx)
    vec = pl.BlockSpec((1, B_HEAD_DIM), lambda b, h, i: (0, 0))
    vmem = (2 * 2 * seq * hw * 2 + 4 * tq * hw * 2 + 2 * tq * seq * 4 + 2 * tq * hw * 4 + 4 * tq * tk * 4) // MIB + 8
    return pl.pallas_call(
        functools.partial(_diffattn_kernel, tk=tk, lam_init=lam_init),
        grid=(batch, B_HEADS, q_tiles),
        in_specs=[
            pl.BlockSpec((None, tq, hw), lambda b, h, i: (0, b * q_tiles + i, h)),
            pl.BlockSpec((None, seq, hw), lambda b, h, i: (1, b, h)),
            pl.BlockSpec((None, seq, hw), lambda b, h, i: (2, b, h)),
            vec, vec, vec, vec,
            pl.BlockSpec((1, B_VALUE_DIM), lambda b, h, i: (0, 0)),
        ],
        out_specs=pl.BlockSpec((tq, hw), lambda b, h, i: (b * q_tiles + i, h)),
        out_shape=jax.ShapeDtypeStruct((n, D), BF16),
        scratch_shapes=[pltpu.VMEM((2, seq // tk, tq, tk), F32),
                        pltpu.VMEM((2, tq, LANE), F32), pltpu.VMEM((2, tq, LANE), F32),
                        pltpu.VMEM((2, tq, tk), BF16), pltpu.VMEM((2, tq, B_VALUE_DIM), F32)],
        compiler_params=_params(("parallel", "parallel", "arbitrary"), vmem),
        name="diffattn",
    )(qkv, qkv, qkv, lam_q1, lam_k1, lam_q2, lam_k2, subln_g)


def _dft_tables(n):
    idx = jnp.arange(n, dtype=jnp.int32)
    ang = ((idx[:, None] * idx[None, :]) % n).astype(F32) * (2.0 * math.pi / n)
    return jnp.cos(ang), jnp.sin(ang)


def _fourier_chan_kernel(x_ref, g_ref, cs_ref, o_ref, h_ref):
    _norm_rows_to(h_ref, x_ref, g_ref)
    for g in range(C_GROUPS):
        cs = slice(g * C_GROUP_DIM, (g + 1) * C_GROUP_DIM)
        ab = jnp.dot(h_ref[:, cs], cs_ref[...], preferred_element_type=F32)
        o_ref[0, :, cs] = ab[:, :C_GROUP_DIM].astype(BF16)
        o_ref[1, :, cs] = ab[:, C_GROUP_DIM:].astype(BF16)


def _fourier_chan(x, g_pre, chan_cs, batch, seq):
    tm = min(ROW_TM, seq)
    tiles = seq // tm
    vmem = (2 * tm * D * 4 + tm * D * 2 + 2 * 2 * tm * D * 2) // MIB + 8
    return pl.pallas_call(
        _fourier_chan_kernel,
        grid=(batch, tiles),
        in_specs=[
            pl.BlockSpec((tm, D), lambda b, i: (b * tiles + i, 0)),
            pl.BlockSpec((1, D), lambda b, i: (0, 0)),
            pl.BlockSpec((C_GROUP_DIM, 2 * C_GROUP_DIM), lambda b, i: (0, 0)),
        ],
        out_specs=pl.BlockSpec((None, 2, tm, D), lambda b, i: (b, 0, i, 0)),
        out_shape=jax.ShapeDtypeStruct((batch, 2, seq, D), BF16),
        scratch_shapes=[pltpu.VMEM((tm, D), BF16)],
        compiler_params=_params(("parallel", "parallel"), vmem),
        name="fourier_chan",
    )(x, g_pre, chan_cs)


def _pos_factor_tables(seq):
    k = jnp.arange(seq, dtype=jnp.int32)

    def cos_sin(j):
        ang = ((j[:, None] * k[None, :]) % seq).astype(F32) * (2.0 * math.pi / seq)
        return jnp.cos(ang), jnp.sin(ang)

    ca, sa = cos_sin(jnp.arange(seq // LANE, dtype=jnp.int32) * LANE)
    cb, sb = cos_sin(jnp.arange(LANE, dtype=jnp.int32))
    return (jnp.concatenate([ca, -sa], axis=1), jnp.concatenate([-sa, -ca], axis=1),
            jnp.concatenate([cb, cb], axis=1), jnp.concatenate([sb, sb], axis=1))


def _fourier_pos_kernel(p_ref, q_ref, u_ref, v_ref, ab_ref, o_ref, t_ref, acc_ref, *, norm):
    k = pl.program_id(2)

    @pl.when(k == 0)
    def _():
        acc_ref[...] = jnp.zeros_like(acc_ref)

    for g in range(t_ref.shape[0] // LANE):
        t_ref[g * LANE:(g + 1) * LANE, :] = (p_ref[g:g + 1, :] * u_ref[...]
                                             + q_ref[g:g + 1, :] * v_ref[...]).astype(BF16)
    acc_ref[...] += jnp.dot(t_ref[...], ab_ref[...], preferred_element_type=F32)

    @pl.when(k == pl.num_programs(2) - 1)
    def _():
        def fn(rs):
            o_ref[rs, :] = (acc_ref[rs, :] * norm).astype(BF16)
        _row_chunks(acc_ref.shape[0], fn)


def _fourier_pos(tables, ab, batch, seq):
    tm = min(DFT_TM, seq)
    tk = min(DFT_TK, 2 * seq)
    tiles = seq // tm
    ab2 = ab.reshape(batch * 2 * seq, D)
    k_tiles = 2 * seq // tk
    groups = tm // LANE
    vmem = (tm * tk * 2 + 4 * LANE * tk * 4 + 2 * tk * D * 2 + tm * D * 4 + 2 * tm * D * 2) // MIB + 8
    coarse = pl.BlockSpec((groups, tk), lambda b, i, k: (i, k))
    fine = pl.BlockSpec((LANE, tk), lambda b, i, k: (0, k))
    return pl.pallas_call(
        functools.partial(_fourier_pos_kernel, norm=(seq * C_GROUP_DIM) ** -0.5),
        grid=(batch, tiles, k_tiles),
        in_specs=[coarse, coarse, fine, fine,
                  pl.BlockSpec((tk, D), lambda b, i, k: (b * k_tiles + k, 0))],
        out_specs=pl.BlockSpec((tm, D), lambda b, i, k: (b * tiles + i, 0)),
        out_shape=jax.ShapeDtypeStruct((batch * seq, D), BF16),
        scratch_shapes=[pltpu.VMEM((tm, tk), BF16), pltpu.VMEM((tm, D), F32)],
        compiler_params=_params(("parallel", "parallel", "arbitrary"), vmem),
        name="fourier_pos",
    )(*tables, ab2)


def _xattn_kernel(x_ref, gpre_ref, wq_ref, kv_ref, wo_ref, gpost_ref, o_ref, h_ref, q_ref, a_ref, y_ref):
    _norm_rows_to(h_ref, x_ref, gpre_ref)
    q_ref[...] = (jnp.dot(h_ref[...], wq_ref[...], preferred_element_type=F32)
                  * (X_HEAD_DIM ** -0.5)).astype(BF16)

    for h in range(X_HEADS):
        cs = slice(h * X_HEAD_DIM, (h + 1) * X_HEAD_DIM)
        vs = slice(X_WIDTH + h * X_HEAD_DIM, X_WIDTH + (h + 1) * X_HEAD_DIM)
        s = lax.dot_general(q_ref[:, cs], kv_ref[:, cs], (((1,), (1,)), ((), ())),
                            preferred_element_type=F32)
        e = jnp.exp(s - jnp.max(s, axis=-1, keepdims=True))
        pv = jnp.dot(e.astype(BF16), kv_ref[:, vs], preferred_element_type=F32)
        a_ref[:, cs] = (pv / jnp.sum(e, axis=-1, keepdims=True)).astype(BF16)
    y_ref[...] = jnp.dot(a_ref[...], wo_ref[...], preferred_element_type=F32)
    _postnorm_residual_to(o_ref, x_ref, y_ref, gpost_ref, 1.0)


def _xattn(x, g_pre, w_q, kv, w_out, g_post, seq):
    n = x.shape[0]
    tm = min(ROW_TM, seq)
    tiles = seq // tm
    vmem = (2 * 2 * tm * D * 4 + tm * D * (2 + 4) + 4 * D * X_WIDTH * 2 + 2 * tm * X_WIDTH * 2) // MIB + 8
    return pl.pallas_call(
        _xattn_kernel,
        grid=(n // tm,),
        in_specs=[
            pl.BlockSpec((tm, D), lambda i: (i, 0)),
            pl.BlockSpec((1, D), lambda i: (0, 0)),
            pl.BlockSpec((D, X_WIDTH), lambda i: (0, 0)),
            pl.BlockSpec((N_MEM, 2 * X_WIDTH), lambda i: (i // tiles, 0)),
            pl.BlockSpec((X_WIDTH, D), lambda i: (0, 0)),
            pl.BlockSpec((1, D), lambda i: (0, 0)),
        ],
        out_specs=pl.BlockSpec((tm, D), lambda i: (i, 0)),
        out_shape=jax.ShapeDtypeStruct((n, D), F32),
        scratch_shapes=[pltpu.VMEM((tm, D), BF16), pltpu.VMEM((tm, X_WIDTH), BF16),
                        pltpu.VMEM((tm, X_WIDTH), BF16), pltpu.VMEM((tm, D), F32)],
        compiler_params=_params(("parallel",), vmem),
        name="xattn",
    )(x, g_pre, w_q, kv, w_out, g_post)


def _prepare(p):
    pad = FF_PAD - D_FF
    w = dict(p)
    w["ffn_w_gate"] = jnp.pad(p["ffn_w_gate"].astype(BF16), ((0, 0), (0, 0), (0, 0), (0, pad)))
    w["ffn_w_up"] = jnp.pad(p["ffn_w_up"].astype(BF16), ((0, 0), (0, 0), (0, 0), (0, pad)))
    w["ffn_w_down"] = jnp.pad(p["ffn_w_down"].astype(BF16), ((0, 0), (0, 0), (0, pad), (0, 0)))
    for name in ("a_w_in", "a_w_s", "a_w_out", "b_w_qkv", "b_w_out", "c_w_out", "x_w_q", "x_w_kv", "x_w_out"):
        w[name] = p[name].astype(BF16)
    w["a_b_s_t"] = jnp.swapaxes(p["a_b_s"], 1, 2)
    cc, sc = _dft_tables(C_GROUP_DIM)
    w["chan_cs"] = jnp.concatenate([cc, sc], axis=1).astype(BF16)
    return w


def _row(v):
    return v.reshape(1, -1)


def _trunk(x3, mem3, w, depth):
    batch, seq, _ = x3.shape
    x = x3.reshape(batch * seq, D)
    mem = mem3.reshape(batch * N_MEM, D)
    n_mixers = 3
    rot = pos_table = None
    for i in range(depth):
        kind, slot = i % n_mixers, i // n_mixers
        pre, post = w["ln_pre"][i], w["ln_post"][i]
        x = _ffn(x, _row(pre[0]), w["ffn_w_gate"][i, 0], w["ffn_w_up"][i, 0], w["ffn_w_down"][i, 0], _row(post[0]))
        if kind == 0:
            uv = _gmlp_in(x, _row(pre[1]), w["a_w_in"][slot], _row(w["a_ln_g"][slot]), _row(w["a_ln_b"][slot]))
            x = _gmlp_out(x, uv, w["a_w_s"][slot], w["a_b_s_t"][slot], w["a_w_out"][slot], _row(post[1]))
        elif kind == 1:
            if rot is None:
                rot = _rotary_tables(seq)
            qkv = _qkv(x, _row(pre[1]), w["b_w_qkv"][slot], rot, seq)
            o = _diffattn(qkv, _row(w["b_lam_q1"][slot]), _row(w["b_lam_k1"][slot]), _row(w["b_lam_q2"][slot]),
                          _row(w["b_lam_k2"][slot]), _row(w["b_subln"][slot]), batch, seq, i)
            x = _proj_res(x, o, w["b_w_out"][slot], _row(post[1]))
        else:
            if pos_table is None:
                pos_table = _pos_factor_tables(seq)
            ab = _fourier_chan(x, _row(pre[1]), w["chan_cs"], batch, seq)
            mixed = _fourier_pos(pos_table, ab, batch, seq)
            x = _proj_res(x, mixed, w["c_w_out"][slot], _row(post[1]))
        kv = _norm_mm(mem, _row(w["ln_mem"][i]), w["x_w_kv"][i])
        x = _xattn(x, _row(pre[2]), w["x_w_q"][i], kv, w["x_w_out"][i], _row(post[2]), seq)
        x = _ffn(x, _row(pre[3]), w["ffn_w_gate"][i, 1], w["ffn_w_up"][i, 1], w["ffn_w_down"][i, 1], _row(post[3]))
    return x.reshape(batch, seq, D)


def kernel(x_prompt, x_sample, mem_prompt, mem_sample, ln_pre, ln_post, ln_mem, ffn_w_gate, ffn_w_up, ffn_w_down, a_w_in, a_ln_g, a_ln_b, a_w_s, a_b_s, a_w_out, b_w_qkv, b_lam_q1, b_lam_k1, b_lam_q2, b_lam_k2, b_subln, b_w_out, c_w_out, x_w_q, x_w_kv, x_w_out):
    params = {
        "ln_pre": ln_pre, "ln_post": ln_post, "ln_mem": ln_mem,
        "ffn_w_gate": ffn_w_gate, "ffn_w_up": ffn_w_up, "ffn_w_down": ffn_w_down,
        "a_w_in": a_w_in, "a_ln_g": a_ln_g, "a_ln_b": a_ln_b, "a_w_s": a_w_s, "a_b_s": a_b_s, "a_w_out": a_w_out,
        "b_w_qkv": b_w_qkv, "b_lam_q1": b_lam_q1, "b_lam_k1": b_lam_k1, "b_lam_q2": b_lam_q2,
        "b_lam_k2": b_lam_k2, "b_subln": b_subln, "b_w_out": b_w_out,
        "c_w_out": c_w_out, "x_w_q": x_w_q, "x_w_kv": x_w_kv, "x_w_out": x_w_out,
    }
    depth = ln_pre.shape[0]
    w = _prepare(params)
    return (_trunk(x_prompt, mem_prompt, w, depth), _trunk(x_sample, mem_sample, w, depth))
```

```python
import functools
import math

import jax
import jax.numpy as jnp
from jax import lax
from jax.experimental import pallas as pl
from jax.experimental.pallas import tpu as pltpu

F32 = jnp.float32
BF16 = jnp.bfloat16

D = 2048
D_FF = 5504
LANE = 128
FF_PAD = 5632
CHUNK = 128
A_GROUPS = 8
A_GROUP_DIM = D // A_GROUPS
B_HEADS = 8
B_HEAD_DIM = 128
B_VALUE_DIM = 256
ROT_DIM = 32
ROPE_THETA = 500000.0
C_GROUPS = 8
C_GROUP_DIM = D // C_GROUPS
X_HEADS = 4
X_HEAD_DIM = 128
X_WIDTH = X_HEADS * X_HEAD_DIM
N_MEM = 256
NORM_EPS = 1e-6
SUBLN_EPS = 1e-5
LN_EPS = 1e-5
MIB = 1024 * 1024
VMEM_CAP_MIB = 60

FFN_TM, FFN_TF = 1024, 512
FFN_PIECES = 8
ROW_TM = 512
ROW_CHUNK = 128
NORM_CHUNK = 256
COL_BLOCK = 512
ATT_TQ, ATT_TK = 256, 512
ATT_SCORE_MIB = 16
ATT_UNROLL = 8
Q_SCALE = B_HEAD_DIM ** -0.5 * math.log2(math.e)
DFT_TM, DFT_TK = 1024, 2048


def _params(semantics, vmem_mib, flags=None):
    return pltpu.CompilerParams(dimension_semantics=semantics, flags=flags,
                                vmem_limit_bytes=min(vmem_mib, VMEM_CAP_MIB) * MIB)


def _row_chunks(n_rows, fn, rc=None):
    rc = min(rc or ROW_CHUNK, n_rows)

    def body(c, carry):
        fn(pl.ds(pl.multiple_of(c * rc, rc), rc))
        return carry

    lax.fori_loop(0, n_rows // rc, body, 0)


def _rms(x, g, eps):
    return x * lax.rsqrt(jnp.mean(x * x, axis=-1, keepdims=True) + eps) * g


def _norm_rows_to(h_ref, x_ref, g_ref):
    def fn(rs):
        h_ref[rs, :] = _rms(x_ref[rs, :], g_ref[...], NORM_EPS).astype(BF16)
    _row_chunks(x_ref.shape[0], fn, NORM_CHUNK)


def _postnorm_residual_to(o_ref, x_ref, y_ref, g_ref, scale):
    def fn(rs):
        o_ref[rs, :] = x_ref[rs, :] + scale * _rms(y_ref[rs, :], g_ref[...], NORM_EPS)
    _row_chunks(x_ref.shape[0], fn, NORM_CHUNK)


def _ffn_kernel(xn_ref, xp_ref, gpre_ref, wg_ref, wu_ref, wd_ref, gpost_ref, o_ref, h_ref, acc_ref,
                *, n_tiles, pieces):
    i = pl.program_id(0)
    f = pl.program_id(1)
    side = i % 2
    main = 1 - side
    pr = xn_ref.shape[0]
    rows = pl.ds(pl.multiple_of(jnp.minimum(f, pieces - 1) * pr, pr), pr)

    def norm_in():
        h_ref[side, rows, :] = _rms(xn_ref[...], gpre_ref[...], NORM_EPS).astype(BF16)

    def finish():
        o_ref[...] = xp_ref[...] + 0.5 * _rms(acc_ref[side, rows, :], gpost_ref[...], NORM_EPS)

    def step(first, with_norms):
        h = h_ref[main]
        gate = jnp.dot(h, wg_ref[...], preferred_element_type=F32)
        up = jnp.dot(h, wu_ref[...], preferred_element_type=F32)
        a = (gate * jax.nn.sigmoid(gate) * up).astype(BF16)
        d = jnp.dot(a, wd_ref[...], preferred_element_type=F32)
        if first:
            acc_ref[main] = d
        else:
            acc_ref[main] += d
        if with_norms:
            norm_in()
            finish()

    has_piece = f < pieces

    @pl.when(jnp.logical_and(i == 0, has_piece))
    def _():
        norm_in()

        @pl.when(f == 0)
        def _():
            acc_ref[...] = jnp.zeros_like(acc_ref)

    in_main = jnp.logical_and(i >= 1, i <= n_tiles)

    @pl.when(jnp.logical_and(in_main, f == 0))
    def _():
        step(True, True)

    @pl.when(jnp.logical_and(in_main, jnp.logical_and(f > 0, has_piece)))
    def _():
        step(False, True)

    @pl.when(jnp.logical_and(in_main, f >= pieces))
    def _():
        step(False, False)

    @pl.when(jnp.logical_and(i == n_tiles + 1, has_piece))
    def _():
        finish()


def _ffn(x, g_pre, wg, wu, wd, g_post):
    n = x.shape[0]
    tm = min(FFN_TM, n)
    tf = FFN_TF
    fp = wg.shape[1]
    n_tiles, nf = n // tm, fp // tf
    pieces = FFN_PIECES
    pr = tm // pieces
    assert pieces <= nf and pr % 16 == 0

    def piece(tile, f):
        return (jnp.where(tile < 0, 0, jnp.minimum(tile, n_tiles - 1) * pieces + jnp.minimum(f, pieces - 1)), 0)

    def wcol(i, f):
        return jnp.where(jnp.logical_and(i >= 1, i <= n_tiles), f, 0)

    vmem = (3 * 2 * pr * D * 4 + 2 * tm * D * (2 + 4) + 2 * 3 * D * tf * 2 + 4 * tm * tf * 4) // MIB + 6
    return pl.pallas_call(
        functools.partial(_ffn_kernel, n_tiles=n_tiles, pieces=pieces),
        grid=(n_tiles + 2, nf),
        in_specs=[
            pl.BlockSpec((pr, D), lambda i, f: piece(i, f)),
            pl.BlockSpec((pr, D), lambda i, f: piece(i - 2, f)),
            pl.BlockSpec((1, D), lambda i, f: (0, 0)),
            pl.BlockSpec((D, tf), lambda i, f: (0, wcol(i, f))),
            pl.BlockSpec((D, tf), lambda i, f: (0, wcol(i, f))),
            pl.BlockSpec((tf, D), lambda i, f: (wcol(i, f), 0)),
            pl.BlockSpec((1, D), lambda i, f: (0, 0)),
        ],
        out_specs=pl.BlockSpec((pr, D), lambda i, f: piece(i - 2, f)),
        out_shape=jax.ShapeDtypeStruct((n, D), F32),
        scratch_shapes=[pltpu.VMEM((2, tm, D), BF16), pltpu.VMEM((2, tm, D), F32)],
        compiler_params=_params(("arbitrary", "arbitrary"), vmem),
        name="ffn",
    )(x, x, g_pre, wg, wu, wd, g_post)


def _norm_mm_kernel(x_ref, g_ref, w_ref, o_ref, h_ref):
    _norm_rows_to(h_ref, x_ref, g_ref)
    o_ref[...] = jnp.dot(h_ref[...], w_ref[...], preferred_element_type=F32).astype(o_ref.dtype)


def _norm_mm(x, g, w):
    n, n_out = x.shape[0], w.shape[1]
    tm = min(ROW_TM, n)
    tn = min(1024, n_out)
    vmem = (2 * tm * D * 4 + tm * D * 2 + 2 * D * tn * 2 + 2 * tm * tn * 2 + tm * tn * 4) // MIB + 6
    return pl.pallas_call(
        _norm_mm_kernel,
        grid=(n_out // tn, n // tm),
        in_specs=[
            pl.BlockSpec((tm, D), lambda j, i: (i, 0)),
            pl.BlockSpec((1, D), lambda j, i: (0, 0)),
            pl.BlockSpec((D, tn), lambda j, i: (0, j)),
        ],
        out_specs=pl.BlockSpec((tm, tn), lambda j, i: (i, j)),
        out_shape=jax.ShapeDtypeStruct((n, n_out), BF16),
        scratch_shapes=[pltpu.VMEM((tm, D), BF16)],
        compiler_params=_params(("parallel", "parallel"), vmem),
        name="norm_mm",
    )(x, g, w)


def _proj_res_kernel(x_ref, a_ref, w_ref, g_ref, o_ref, y_ref):
    y_ref[...] = jnp.dot(a_ref[...], w_ref[...], preferred_element_type=F32)
    _postnorm_residual_to(o_ref, x_ref, y_ref, g_ref, 1.0)


def _proj_res(x, a, w, g_post):
    n, k = a.shape
    tm = min(ROW_TM, n)
    vmem = (2 * 2 * tm * D * 4 + 2 * tm * k * 2 + 2 * k * D * 2 + tm * D * 4) // MIB + 6
    return pl.pallas_call(
        _proj_res_kernel,
        grid=(n // tm,),
        in_specs=[
            pl.BlockSpec((tm, D), lambda i: (i, 0)),
            pl.BlockSpec((tm, k), lambda i: (i, 0)),
            pl.BlockSpec((k, D), lambda i: (0, 0)),
            pl.BlockSpec((1, D), lambda i: (0, 0)),
        ],
        out_specs=pl.BlockSpec((tm, D), lambda i: (i, 0)),
        out_shape=jax.ShapeDtypeStruct((n, D), F32),
        scratch_shapes=[pltpu.VMEM((tm, D), F32)],
        compiler_params=_params(("parallel",), vmem),
        name="proj_res",
    )(x, a, w, g_post)


def _gelu(y):
    return 0.5 * y * (1.0 + lax.erf(y * (0.5 ** 0.5)))


def _gmlp_in_kernel(x_ref, g_ref, w_ref, lg_ref, lb_ref, o_ref, h_ref, y_ref):
    j = pl.program_id(0)
    _norm_rows_to(h_ref, x_ref, g_ref)

    def gelu_block(c):
        cs = slice(c * COL_BLOCK, (c + 1) * COL_BLOCK)
        return cs, _gelu(jnp.dot(h_ref[...], w_ref[:, cs], preferred_element_type=F32))

    @pl.when(j == 0)
    def _():
        for c in range(D // COL_BLOCK):
            cs, y = gelu_block(c)
            o_ref[:, cs] = y.astype(BF16)

    @pl.when(j == 1)
    def _():
        for c in range(D // COL_BLOCK):
            cs, y = gelu_block(c)
            y_ref[:, cs] = y

        def fn(rs):
            v = y_ref[rs, :]
            vc = v - jnp.mean(v, axis=-1, keepdims=True)
            var = jnp.mean(vc * vc, axis=-1, keepdims=True)
            o_ref[rs, :] = (vc * lax.rsqrt(var + LN_EPS) * lg_ref[...] + lb_ref[...]).astype(BF16)
        _row_chunks(x_ref.shape[0], fn)


def _gmlp_in(x, g_pre, w_in, ln_g, ln_b):
    n = x.shape[0]
    tm = min(ROW_TM, n)
    vmem = (2 * tm * D * 4 + tm * D * 2 + 2 * D * D * 2 + 2 * tm * D * 2 + tm * D * 4) // MIB + 6
    return pl.pallas_call(
        _gmlp_in_kernel,
        grid=(2, n // tm),
        in_specs=[
            pl.BlockSpec((tm, D), lambda j, i: (i, 0)),
            pl.BlockSpec((1, D), lambda j, i: (0, 0)),
            pl.BlockSpec((D, D), lambda j, i: (0, j)),
            pl.BlockSpec((1, D), lambda j, i: (0, 0)),
            pl.BlockSpec((1, D), lambda j, i: (0, 0)),
        ],
        out_specs=pl.BlockSpec((None, tm, D), lambda j, i: (j, i, 0)),
        out_shape=jax.ShapeDtypeStruct((2, n, D), BF16),
        scratch_shapes=[pltpu.VMEM((tm, D), BF16), pltpu.VMEM((tm, D), F32)],
        compiler_params=_params(("parallel", "parallel"), vmem),
        name="gmlp_in",
    )(x, g_pre, w_in, ln_g, ln_b)


def _gmlp_out_kernel(x_ref, u_ref, v_ref, ws_ref, bs_ref, w_ref, g_ref, o_ref, a_ref, y_ref):
    def fn(rs):
        for g in range(A_GROUPS):
            cs = slice(g * A_GROUP_DIM, (g + 1) * A_GROUP_DIM)
            mixed = jnp.dot(ws_ref[g], v_ref[rs, cs], preferred_element_type=F32) + bs_ref[:, g:g + 1]
            a_ref[rs, cs] = (u_ref[rs, cs].astype(F32) * mixed).astype(BF16)
    _row_chunks(x_ref.shape[0], fn)
    y_ref[...] = jnp.dot(a_ref[...], w_ref[...], preferred_element_type=F32)
    _postnorm_residual_to(o_ref, x_ref, y_ref, g_ref, 1.0)


def _gmlp_out(x, uv, w_s, b_s_t, w_out, g_post):
    n = x.shape[0]
    tm = min(ROW_TM, n)
    assert ROW_CHUNK == CHUNK and tm % CHUNK == 0
    vmem = (2 * 2 * tm * D * 4 + 2 * 2 * tm * D * 2 + 2 * D * D * 2 + tm * D * (2 + 4)) // MIB + 6
    return pl.pallas_call(
        _gmlp_out_kernel,
        grid=(n // tm,),
        in_specs=[
            pl.BlockSpec((tm, D), lambda i: (i, 0)),
            pl.BlockSpec((None, tm, D), lambda i: (0, i, 0)),
            pl.BlockSpec((None, tm, D), lambda i: (1, i, 0)),
            pl.BlockSpec((A_GROUPS, CHUNK, CHUNK), lambda i: (0, 0, 0)),
            pl.BlockSpec((CHUNK, A_GROUPS), lambda i: (0, 0)),
            pl.BlockSpec((D, D), lambda i: (0, 0)),
            pl.BlockSpec((1, D), lambda i: (0, 0)),
        ],
        out_specs=pl.BlockSpec((tm, D), lambda i: (i, 0)),
        out_shape=jax.ShapeDtypeStruct((n, D), F32),
        scratch_shapes=[pltpu.VMEM((tm, D), BF16), pltpu.VMEM((tm, D), F32)],
        compiler_params=_params(("parallel",), vmem),
        name="gmlp_out",
    )(x, uv, uv, w_s, b_s_t, w_out, g_post)


def _rotary_tables(s):
    half = ROT_DIM // 2
    inv_freq = 1.0 / (ROPE_THETA ** (jnp.arange(0, ROT_DIM, 2, dtype=F32) / ROT_DIM))
    ang = jnp.arange(s, dtype=F32)[:, None] * inv_freq[None, :]
    cos, sin = jnp.cos(ang), jnp.sin(ang)
    rest = B_HEAD_DIM - ROT_DIM
    ct = jnp.concatenate([cos, cos, jnp.ones((s, rest), F32)], axis=1)
    sa = jnp.concatenate([-sin, jnp.zeros((s, half + rest), F32)], axis=1)
    sb = jnp.concatenate([jnp.zeros((s, half), F32), sin, jnp.zeros((s, rest), F32)], axis=1)
    return ct, sa, sb


def _qkv_kernel(x_ref, g_ref, w_ref, ct_ref, sa_ref, sb_ref, qv_ref, kt_ref, h_ref, t_ref):
    j = pl.program_id(0)
    _norm_rows_to(h_ref, x_ref, g_ref)

    def block(c):
        return jnp.dot(h_ref[...], w_ref[:, c * COL_BLOCK:(c + 1) * COL_BLOCK], preferred_element_type=F32)

    def rotary_blocks(scale, write):
        t_ref[0] = ct_ref[...] * scale
        t_ref[1] = sa_ref[...] * scale
        t_ref[2] = sb_ref[...] * scale
        for c in range(D // COL_BLOCK):
            y = block(c)
            for k in range(COL_BLOCK // LANE):
                yc = y[:, k * LANE:(k + 1) * LANE]
                rot = (yc * t_ref[0] + pltpu.roll(yc, LANE - ROT_DIM // 2, 1) * t_ref[1]
                       + pltpu.roll(yc, ROT_DIM // 2, 1) * t_ref[2])
                write(c * COL_BLOCK + k * LANE, rot)

    @pl.when(j == 0)
    def _():
        def write(lo, rot):
            qv_ref[:, lo:lo + LANE] = rot.astype(BF16)
        rotary_blocks(Q_SCALE, write)

    @pl.when(j == 1)
    def _():
        def write(lo, rot):
            kt_ref[lo:lo + LANE, :] = rot.T.astype(BF16)
        rotary_blocks(1.0, write)

    @pl.when(j == 2)
    def _():
        for c in range(D // COL_BLOCK):
            qv_ref[:, c * COL_BLOCK:(c + 1) * COL_BLOCK] = block(c).astype(BF16)


def _qkv(x, g_pre, w_qkv, tables, batch, seq):
    n = x.shape[0]
    tm = min(ATT_TK, seq)
    pos_tiles = seq // tm
    tiles = n // tm
    ct, sa, sb = tables
    vmem = (2 * tm * D * 4 + tm * D * 2 + 2 * D * D * 2 + 4 * tm * D * 2 + tm * COL_BLOCK * 4) // MIB + 8
    tab = pl.BlockSpec((tm, LANE), lambda j, i: (i % pos_tiles, 0))

    def qv_block(j, i):
        return (j // 2, jnp.where(j == 1, tiles - 1, i), 0)

    def kt_block(j, i):
        t = jnp.where(j == 0, 0, jnp.where(j == 1, i, tiles - 1))
        return (t // pos_tiles, t % pos_tiles, 0, 0)

    return pl.pallas_call(
        _qkv_kernel,
        grid=(3, tiles),
        in_specs=[
            pl.BlockSpec((tm, D), lambda j, i: (i, 0)),
            pl.BlockSpec((1, D), lambda j, i: (0, 0)),
            pl.BlockSpec((D, D), lambda j, i: (0, j)),
            tab, tab, tab,
        ],
        out_specs=[pl.BlockSpec((None, tm, D), qv_block),
                   pl.BlockSpec((None, None, D, tm), kt_block)],
        out_shape=[jax.ShapeDtypeStruct((2, n, D), BF16),
                   jax.ShapeDtypeStruct((batch, pos_tiles, D, tm), BF16)],
        scratch_shapes=[pltpu.VMEM((tm, D), BF16), pltpu.VMEM((3, tm, LANE), F32)],
        compiler_params=_params(("arbitrary", "arbitrary"), vmem),
        name="qkv",
    )(x, g_pre, w_qkv, ct, sa, sb)


def _diffattn_kernel(q_ref, kt_ref, v_ref, lq1_ref, lk1_ref, lq2_ref, lk2_ref, sg_ref, o_ref,
                     s_ref, m_ref, l_ref, p_ref, acc_ref, *, tk, lam_init):
    seq = v_ref.shape[0]
    lane_blocks = tk // LANE
    unroll = max(1, min(ATT_UNROLL, seq // tk // 2))
    lam = (jnp.exp(jnp.sum(lq1_ref[...] * lk1_ref[...], axis=-1, keepdims=True))
           - jnp.exp(jnp.sum(lq2_ref[...] * lk2_ref[...], axis=-1, keepdims=True)) + lam_init)
    m_ref[...] = jnp.full_like(m_ref, -jnp.inf)
    l_ref[...] = jnp.zeros_like(l_ref)
    acc_ref[...] = jnp.zeros_like(acc_ref)

    def scores(t, carry):
        for c in range(2):
            cs = slice(c * B_HEAD_DIM, (c + 1) * B_HEAD_DIM)
            s = jnp.dot(q_ref[:, cs], kt_ref[t, cs, :], preferred_element_type=F32)
            s_ref[c, t] = s
            m = m_ref[c]
            for j in range(lane_blocks):
                m = jnp.maximum(m, s[:, j * LANE:(j + 1) * LANE])
            m_ref[c] = m
        return carry

    lax.fori_loop(0, seq // tk, scores, 0, unroll=unroll)
    for c in range(2):
        m_ref[c] = jnp.broadcast_to(jnp.max(m_ref[c], axis=-1, keepdims=True), m_ref.shape[1:])

    def weights(t, carry):
        ks = pl.ds(pl.multiple_of(t * tk, tk), tk)
        for c in range(2):
            m = m_ref[c]
            l = l_ref[c]
            for j in range(lane_blocks):
                ls = slice(j * LANE, (j + 1) * LANE)
                p = jnp.exp2(s_ref[c, t, :, ls] - m)
                l = l + p
                p_ref[c, :, ls] = p.astype(BF16)
            l_ref[c] = l
            acc_ref[c] += jnp.dot(p_ref[c], v_ref[ks, :], preferred_element_type=F32)
        return carry

    lax.fori_loop(0, seq // tk, weights, 0, unroll=unroll)
    l0 = jnp.sum(l_ref[0], axis=-1, keepdims=True)
    l1 = jnp.sum(l_ref[1], axis=-1, keepdims=True)
    o = acc_ref[0] / l0 - lam * (acc_ref[1] / l1)
    o_ref[...] = (_rms(o, sg_ref[...], SUBLN_EPS) * (1.0 - lam_init)).astype(BF16)


def _diffattn(qv, kt, lam_q1, lam_k1, lam_q2, lam_k2, subln_g, batch, seq, layer_idx):
    n = batch * seq
    tq = min(ATT_TQ, seq, ATT_SCORE_MIB * MIB // (8 * seq))
    tk = min(ATT_TK, seq)
    q_tiles = seq // tq
    hw = 2 * B_HEAD_DIM
    lam_init = 0.8 - 0.6 * math.exp(-0.3 * layer_idx)
    vec = pl.BlockSpec((1, B_HEAD_DIM), lambda b, h, i: (0, 0))
    vmem = (2 * 2 * seq * hw * 2 + 4 * tq * hw * 2 + 2 * tq * seq * 4 + 2 * tq * hw * 4 + 4 * tq * tk * 4) // MIB + 8
    return pl.pallas_call(
        functools.partial(_diffattn_kernel, tk=tk, lam_init=lam_init),
        grid=(batch, B_HEADS, q_tiles),
        in_specs=[
            pl.BlockSpec((None, tq, hw), lambda b, h, i: (0, b * q_tiles + i, h)),
            pl.BlockSpec((None, seq // tk, hw, tk), lambda b, h, i: (b, 0, h, 0)),
            pl.BlockSpec((None, seq, hw), lambda b, h, i: (1, b, h)),
            vec, vec, vec, vec,
            pl.BlockSpec((1, B_VALUE_DIM), lambda b, h, i: (0, 0)),
        ],
        out_specs=pl.BlockSpec((tq, hw), lambda b, h, i: (b * q_tiles + i, h)),
        out_shape=jax.ShapeDtypeStruct((n, D), BF16),
        scratch_shapes=[pltpu.VMEM((2, seq // tk, tq, tk), F32),
                        pltpu.VMEM((2, tq, LANE), F32), pltpu.VMEM((2, tq, LANE), F32),
                        pltpu.VMEM((2, tq, tk), BF16), pltpu.VMEM((2, tq, B_VALUE_DIM), F32)],
        compiler_params=_params(("parallel", "parallel", "arbitrary"), vmem),
        name="diffattn",
    )(qv, kt, qv, lam_q1, lam_k1, lam_q2, lam_k2, subln_g)


def _dft_tables(n):
    idx = jnp.arange(n, dtype=jnp.int32)
    ang = ((idx[:, None] * idx[None, :]) % n).astype(F32) * (2.0 * math.pi / n)
    return jnp.cos(ang), jnp.sin(ang)


def _fourier_chan_kernel(x_ref, g_ref, cs_ref, o_ref, h_ref):
    _norm_rows_to(h_ref, x_ref, g_ref)
    for g in range(C_GROUPS):
        cs = slice(g * C_GROUP_DIM, (g + 1) * C_GROUP_DIM)
        ab = jnp.dot(h_ref[:, cs], cs_ref[...], preferred_element_type=F32)
        o_ref[0, :, cs] = ab[:, :C_GROUP_DIM].astype(BF16)
        o_ref[1, :, cs] = ab[:, C_GROUP_DIM:].astype(BF16)


def _fourier_chan(x, g_pre, chan_cs, batch, seq):
    tm = min(ROW_TM, seq)
    tiles = seq // tm
    vmem = (2 * tm * D * 4 + tm * D * 2 + 2 * 2 * tm * D * 2) // MIB + 8
    return pl.pallas_call(
        _fourier_chan_kernel,
        grid=(batch, tiles),
        in_specs=[
            pl.BlockSpec((tm, D), lambda b, i: (b * tiles + i, 0)),
            pl.BlockSpec((1, D), lambda b, i: (0, 0)),
            pl.BlockSpec((C_GROUP_DIM, 2 * C_GROUP_DIM), lambda b, i: (0, 0)),
        ],
        out_specs=pl.BlockSpec((None, 2, tm, D), lambda b, i: (b, 0, i, 0)),
        out_shape=jax.ShapeDtypeStruct((batch, 2, seq, D), BF16),
        scratch_shapes=[pltpu.VMEM((tm, D), BF16)],
        compiler_params=_params(("parallel", "parallel"), vmem),
        name="fourier_chan",
    )(x, g_pre, chan_cs)


def _pos_factor_tables(seq):
    k = jnp.arange(seq, dtype=jnp.int32)

    def cos_sin(j):
        ang = ((j[:, None] * k[None, :]) % seq).astype(F32) * (2.0 * math.pi / seq)
        return jnp.cos(ang), jnp.sin(ang)

    ca, sa = cos_sin(jnp.arange(seq // LANE, dtype=jnp.int32) * LANE)
    cb, sb = cos_sin(jnp.arange(LANE, dtype=jnp.int32))
    return (jnp.concatenate([ca, -sa], axis=1), jnp.concatenate([-sa, -ca], axis=1),
            jnp.concatenate([cb, cb], axis=1), jnp.concatenate([sb, sb], axis=1))


def _fourier_pos_kernel(p_ref, q_ref, u_ref, v_ref, ab_ref, o_ref, t_ref, acc_ref, *, norm):
    k = pl.program_id(2)

    @pl.when(k == 0)
    def _():
        acc_ref[...] = jnp.zeros_like(acc_ref)

    for g in range(t_ref.shape[0] // LANE):
        t_ref[g * LANE:(g + 1) * LANE, :] = (p_ref[g:g + 1, :] * u_ref[...]
                                             + q_ref[g:g + 1, :] * v_ref[...]).astype(BF16)
    acc_ref[...] += jnp.dot(t_ref[...], ab_ref[...], preferred_element_type=F32)

    @pl.when(k == pl.num_programs(2) - 1)
    def _():
        def fn(rs):
            o_ref[rs, :] = (acc_ref[rs, :] * norm).astype(BF16)
        _row_chunks(acc_ref.shape[0], fn)


def _fourier_pos(tables, ab, batch, seq):
    tm = min(DFT_TM, seq)
    tk = min(DFT_TK, 2 * seq)
    tiles = seq // tm
    ab2 = ab.reshape(batch * 2 * seq, D)
    k_tiles = 2 * seq // tk
    groups = tm // LANE
    vmem = (tm * tk * 2 + 4 * LANE * tk * 4 + 2 * tk * D * 2 + tm * D * 4 + 2 * tm * D * 2) // MIB + 8
    coarse = pl.BlockSpec((groups, tk), lambda b, i, k: (i, k))
    fine = pl.BlockSpec((LANE, tk), lambda b, i, k: (0, k))
    return pl.pallas_call(
        functools.partial(_fourier_pos_kernel, norm=(seq * C_GROUP_DIM) ** -0.5),
        grid=(batch, tiles, k_tiles),
        in_specs=[coarse, coarse, fine, fine,
                  pl.BlockSpec((tk, D), lambda b, i, k: (b * k_tiles + k, 0))],
        out_specs=pl.BlockSpec((tm, D), lambda b, i, k: (b * tiles + i, 0)),
        out_shape=jax.ShapeDtypeStruct((batch * seq, D), BF16),
        scratch_shapes=[pltpu.VMEM((tm, tk), BF16), pltpu.VMEM((tm, D), F32)],
        compiler_params=_params(("parallel", "parallel", "arbitrary"), vmem),
        name="fourier_pos",
    )(*tables, ab2)


def _xattn_kernel(x_ref, gpre_ref, wq_ref, kv_ref, wo_ref, gpost_ref, o_ref, h_ref, q_ref, a_ref, y_ref):
    _norm_rows_to(h_ref, x_ref, gpre_ref)
    q_ref[...] = (jnp.dot(h_ref[...], wq_ref[...], preferred_element_type=F32)
                  * (X_HEAD_DIM ** -0.5)).astype(BF16)

    for h in range(X_HEADS):
        cs = slice(h * X_HEAD_DIM, (h + 1) * X_HEAD_DIM)
        vs = slice(X_WIDTH + h * X_HEAD_DIM, X_WIDTH + (h + 1) * X_HEAD_DIM)
        s = lax.dot_general(q_ref[:, cs], kv_ref[:, cs], (((1,), (1,)), ((), ())),
                            preferred_element_type=F32)
        e = jnp.exp(s - jnp.max(s, axis=-1, keepdims=True))
        pv = jnp.dot(e.astype(BF16), kv_ref[:, vs], preferred_element_type=F32)
        a_ref[:, cs] = (pv / jnp.sum(e, axis=-1, keepdims=True)).astype(BF16)
    y_ref[...] = jnp.dot(a_ref[...], wo_ref[...], preferred_element_type=F32)
    _postnorm_residual_to(o_ref, x_ref, y_ref, gpost_ref, 1.0)


def _xattn(x, g_pre, w_q, kv, w_out, g_post, seq):
    n = x.shape[0]
    tm = min(ROW_TM, seq)
    tiles = seq // tm
    vmem = (2 * 2 * tm * D * 4 + tm * D * (2 + 4) + 4 * D * X_WIDTH * 2 + 2 * tm * X_WIDTH * 2) // MIB + 8
    return pl.pallas_call(
        _xattn_kernel,
        grid=(n // tm,),
        in_specs=[
            pl.BlockSpec((tm, D), lambda i: (i, 0)),
            pl.BlockSpec((1, D), lambda i: (0, 0)),
            pl.BlockSpec((D, X_WIDTH), lambda i: (0, 0)),
            pl.BlockSpec((N_MEM, 2 * X_WIDTH), lambda i: (i // tiles, 0)),
            pl.BlockSpec((X_WIDTH, D), lambda i: (0, 0)),
            pl.BlockSpec((1, D), lambda i: (0, 0)),
        ],
        out_specs=pl.BlockSpec((tm, D), lambda i: (i, 0)),
        out_shape=jax.ShapeDtypeStruct((n, D), F32),
        scratch_shapes=[pltpu.VMEM((tm, D), BF16), pltpu.VMEM((tm, X_WIDTH), BF16),
                        pltpu.VMEM((tm, X_WIDTH), BF16), pltpu.VMEM((tm, D), F32)],
        compiler_params=_params(("parallel",), vmem),
        name="xattn",
    )(x, g_pre, w_q, kv, w_out, g_post)


def _prepare(p):
    pad = FF_PAD - D_FF
    w = dict(p)
    w["ffn_w_gate"] = jnp.pad(p["ffn_w_gate"].astype(BF16), ((0, 0), (0, 0), (0, 0), (0, pad)))
    w["ffn_w_up"] = jnp.pad(p["ffn_w_up"].astype(BF16), ((0, 0), (0, 0), (0, 0), (0, pad)))
    w["ffn_w_down"] = jnp.pad(p["ffn_w_down"].astype(BF16), ((0, 0), (0, 0), (0, pad), (0, 0)))
    for name in ("a_w_in", "a_w_s", "a_w_out", "b_w_qkv", "b_w_out", "c_w_out", "x_w_q", "x_w_kv", "x_w_out"):
        w[name] = p[name].astype(BF16)
    w["a_b_s_t"] = jnp.swapaxes(p["a_b_s"], 1, 2)
    cc, sc = _dft_tables(C_GROUP_DIM)
    w["chan_cs"] = jnp.concatenate([cc, sc], axis=1).astype(BF16)
    return w


def _row(v):
    return v.reshape(1, -1)


def _trunk(x3, mem3, w, depth):
    batch, seq, _ = x3.shape
    x = x3.reshape(batch * seq, D)
    mem = mem3.reshape(batch * N_MEM, D)
    n_mixers = 3
    rot = pos_table = None
    for i in range(depth):
        kind, slot = i % n_mixers, i // n_mixers
        pre, post = w["ln_pre"][i], w["ln_post"][i]
        x = _ffn(x, _row(pre[0]), w["ffn_w_gate"][i, 0], w["ffn_w_up"][i, 0], w["ffn_w_down"][i, 0], _row(post[0]))
        if kind == 0:
            uv = _gmlp_in(x, _row(pre[1]), w["a_w_in"][slot], _row(w["a_ln_g"][slot]), _row(w["a_ln_b"][slot]))
            x = _gmlp_out(x, uv, w["a_w_s"][slot], w["a_b_s_t"][slot], w["a_w_out"][slot], _row(post[1]))
        elif kind == 1:
            if rot is None:
                rot = _rotary_tables(seq)
            qv, kt = _qkv(x, _row(pre[1]), w["b_w_qkv"][slot], rot, batch, seq)
            o = _diffattn(qv, kt, _row(w["b_lam_q1"][slot]), _row(w["b_lam_k1"][slot]), _row(w["b_lam_q2"][slot]),
                          _row(w["b_lam_k2"][slot]), _row(w["b_subln"][slot]), batch, seq, i)
            x = _proj_res(x, o, w["b_w_out"][slot], _row(post[1]))
        else:
            if pos_table is None:
                pos_table = _pos_factor_tables(seq)
            ab = _fourier_chan(x, _row(pre[1]), w["chan_cs"], batch, seq)
            mixed = _fourier_pos(pos_table, ab, batch, seq)
            x = _proj_res(x, mixed, w["c_w_out"][slot], _row(post[1]))
        kv = _norm_mm(mem, _row(w["ln_mem"][i]), w["x_w_kv"][i])
        x = _xattn(x, _row(pre[2]), w["x_w_q"][i], kv, w["x_w_out"][i], _row(post[2]), seq)
        x = _ffn(x, _row(pre[3]), w["ffn_w_gate"][i, 1], w["ffn_w_up"][i, 1], w["ffn_w_down"][i, 1], _row(post[3]))
    return x.reshape(batch, seq, D)


def kernel(x_prompt, x_sample, mem_prompt, mem_sample, ln_pre, ln_post, ln_mem, ffn_w_gate, ffn_w_up, ffn_w_down, a_w_in, a_ln_g, a_ln_b, a_w_s, a_b_s, a_w_out, b_w_qkv, b_lam_q1, b_lam_k1, b_lam_q2, b_lam_k2, b_subln, b_w_out, c_w_out, x_w_q, x_w_kv, x_w_out):
    params = {
        "ln_pre": ln_pre, "ln_post": ln_post, "ln_mem": ln_mem,
        "ffn_w_gate": ffn_w_gate, "ffn_w_up": ffn_w_up, "ffn_w_down": ffn_w_down,
        "a_w_in": a_w_in, "a_ln_g": a_ln_g, "a_ln_b": a_ln_b, "a_w_s": a_w_s, "a_b_s": a_b_s, "a_w_out": a_w_out,
        "b_w_qkv": b_w_qkv, "b_lam_q1": b_lam_q1, "b_lam_k1": b_lam_k1, "b_lam_q2": b_lam_q2,
        "b_lam_k2": b_lam_k2, "b_subln": b_subln, "b_w_out": b_w_out,
        "c_w_out": c_w_out, "x_w_q": x_w_q, "x_w_kv": x_w_kv, "x_w_out": x_w_out,
    }
    depth = ln_pre.shape[0]
    w = _prepare(params)
    return (_trunk(x_prompt, mem_prompt, w, depth), _trunk(x_sample, mem_sample, w, depth))
```

```python
import functools
import math

import jax
import jax.numpy as jnp
from jax import lax
from jax.experimental import pallas as pl
from jax.experimental.pallas import tpu as pltpu

F32 = jnp.float32
BF16 = jnp.bfloat16

D = 2048
D_FF = 5504
LANE = 128
FF_PAD = 5632
CHUNK = 128
A_GROUPS = 8
A_GROUP_DIM = D // A_GROUPS
B_HEADS = 8
B_HEAD_DIM = 128
B_VALUE_DIM = 256
ROT_DIM = 32
ROPE_THETA = 500000.0
C_GROUPS = 8
C_GROUP_DIM = D // C_GROUPS
X_HEADS = 4
X_HEAD_DIM = 128
X_WIDTH = X_HEADS * X_HEAD_DIM
N_MEM = 256
NORM_EPS = 1e-6
SUBLN_EPS = 1e-5
LN_EPS = 1e-5
MIB = 1024 * 1024
VMEM_CAP_MIB = 60

FFN_TM, FFN_TF = 1024, 512
FFN_PIECES = 8
ROW_TM = 512
ROW_CHUNK = 128
NORM_CHUNK = 256
COL_BLOCK = 512
ATT_TQ = 512
ATT_SCORE_MIB = 16
ATT_TILE = 128 * 1024
ATT_UNROLL = 8
Q_SCALE = B_HEAD_DIM ** -0.5 * math.log2(math.e)
DFT_TM, DFT_TK = 1024, 2048


def _params(semantics, vmem_mib, flags=None):
    return pltpu.CompilerParams(dimension_semantics=semantics, flags=flags,
                                vmem_limit_bytes=min(vmem_mib, VMEM_CAP_MIB) * MIB)


def _row_chunks(n_rows, fn, rc=None):
    rc = min(rc or ROW_CHUNK, n_rows)

    def body(c, carry):
        fn(pl.ds(pl.multiple_of(c * rc, rc), rc))
        return carry

    lax.fori_loop(0, n_rows // rc, body, 0)


def _rms(x, g, eps):
    return x * lax.rsqrt(jnp.mean(x * x, axis=-1, keepdims=True) + eps) * g


def _norm_rows_to(h_ref, x_ref, g_ref):
    def fn(rs):
        h_ref[rs, :] = _rms(x_ref[rs, :], g_ref[...], NORM_EPS).astype(BF16)
    _row_chunks(x_ref.shape[0], fn, NORM_CHUNK)


def _postnorm_residual_to(o_ref, x_ref, y_ref, g_ref, scale):
    def fn(rs):
        o_ref[rs, :] = x_ref[rs, :] + scale * _rms(y_ref[rs, :], g_ref[...], NORM_EPS)
    _row_chunks(x_ref.shape[0], fn, NORM_CHUNK)


def _ffn_kernel(xn_ref, xp_ref, gpre_ref, wg_ref, wu_ref, wd_ref, gpost_ref, o_ref, h_ref, acc_ref,
                *, n_tiles, pieces):
    i = pl.program_id(0)
    f = pl.program_id(1)
    side = i % 2
    main = 1 - side
    pr = xn_ref.shape[0]
    rows = pl.ds(pl.multiple_of(jnp.minimum(f, pieces - 1) * pr, pr), pr)

    def norm_in():
        h_ref[side, rows, :] = _rms(xn_ref[...], gpre_ref[...], NORM_EPS).astype(BF16)

    def finish():
        o_ref[...] = xp_ref[...] + 0.5 * _rms(acc_ref[side, rows, :], gpost_ref[...], NORM_EPS)

    def step(first, with_norms):
        h = h_ref[main]
        gate = jnp.dot(h, wg_ref[...], preferred_element_type=F32)
        up = jnp.dot(h, wu_ref[...], preferred_element_type=F32)
        a = (gate * jax.nn.sigmoid(gate) * up).astype(BF16)
        d = jnp.dot(a, wd_ref[...], preferred_element_type=F32)
        if first:
            acc_ref[main] = d
        else:
            acc_ref[main] += d
        if with_norms:
            norm_in()
            finish()

    has_piece = f < pieces

    @pl.when(jnp.logical_and(i == 0, has_piece))
    def _():
        norm_in()

        @pl.when(f == 0)
        def _():
            acc_ref[...] = jnp.zeros_like(acc_ref)

    in_main = jnp.logical_and(i >= 1, i <= n_tiles)

    @pl.when(jnp.logical_and(in_main, f == 0))
    def _():
        step(True, True)

    @pl.when(jnp.logical_and(in_main, jnp.logical_and(f > 0, has_piece)))
    def _():
        step(False, True)

    @pl.when(jnp.logical_and(in_main, f >= pieces))
    def _():
        step(False, False)

    @pl.when(jnp.logical_and(i == n_tiles + 1, has_piece))
    def _():
        finish()


def _ffn(x, g_pre, wg, wu, wd, g_post):
    n = x.shape[0]
    tm = min(FFN_TM, n)
    tf = FFN_TF
    fp = wg.shape[1]
    n_tiles, nf = n // tm, fp // tf
    pieces = FFN_PIECES
    pr = tm // pieces
    assert pieces <= nf and pr % 16 == 0

    def piece(tile, f):
        return (jnp.where(tile < 0, 0, jnp.minimum(tile, n_tiles - 1) * pieces + jnp.minimum(f, pieces - 1)), 0)

    def wcol(i, f):
        return jnp.where(jnp.logical_and(i >= 1, i <= n_tiles), f, 0)

    vmem = (3 * 2 * pr * D * 4 + 2 * tm * D * (2 + 4) + 2 * 3 * D * tf * 2 + 4 * tm * tf * 4) // MIB + 6
    return pl.pallas_call(
        functools.partial(_ffn_kernel, n_tiles=n_tiles, pieces=pieces),
        grid=(n_tiles + 2, nf),
        in_specs=[
            pl.BlockSpec((pr, D), lambda i, f: piece(i, f)),
            pl.BlockSpec((pr, D), lambda i, f: piece(i - 2, f)),
            pl.BlockSpec((1, D), lambda i, f: (0, 0)),
            pl.BlockSpec((D, tf), lambda i, f: (0, wcol(i, f))),
            pl.BlockSpec((D, tf), lambda i, f: (0, wcol(i, f))),
            pl.BlockSpec((tf, D), lambda i, f: (wcol(i, f), 0)),
            pl.BlockSpec((1, D), lambda i, f: (0, 0)),
        ],
        out_specs=pl.BlockSpec((pr, D), lambda i, f: piece(i - 2, f)),
        out_shape=jax.ShapeDtypeStruct((n, D), F32),
        scratch_shapes=[pltpu.VMEM((2, tm, D), BF16), pltpu.VMEM((2, tm, D), F32)],
        compiler_params=_params(("arbitrary", "arbitrary"), vmem),
        name="ffn",
    )(x, x, g_pre, wg, wu, wd, g_post)


def _norm_mm_kernel(x_ref, g_ref, w_ref, o_ref, h_ref):
    _norm_rows_to(h_ref, x_ref, g_ref)
    o_ref[...] = jnp.dot(h_ref[...], w_ref[...], preferred_element_type=F32).astype(o_ref.dtype)


def _norm_mm(x, g, w):
    n, n_out = x.shape[0], w.shape[1]
    tm = min(ROW_TM, n)
    tn = min(1024, n_out)
    vmem = (2 * tm * D * 4 + tm * D * 2 + 2 * D * tn * 2 + 2 * tm * tn * 2 + tm * tn * 4) // MIB + 6
    return pl.pallas_call(
        _norm_mm_kernel,
        grid=(n_out // tn, n // tm),
        in_specs=[
            pl.BlockSpec((tm, D), lambda j, i: (i, 0)),
            pl.BlockSpec((1, D), lambda j, i: (0, 0)),
            pl.BlockSpec((D, tn), lambda j, i: (0, j)),
        ],
        out_specs=pl.BlockSpec((tm, tn), lambda j, i: (i, j)),
        out_shape=jax.ShapeDtypeStruct((n, n_out), BF16),
        scratch_shapes=[pltpu.VMEM((tm, D), BF16)],
        compiler_params=_params(("parallel", "parallel"), vmem),
        name="norm_mm",
    )(x, g, w)


def _proj_res_kernel(x_ref, a_ref, w_ref, g_ref, o_ref, y_ref):
    y_ref[...] = jnp.dot(a_ref[...], w_ref[...], preferred_element_type=F32)
    _postnorm_residual_to(o_ref, x_ref, y_ref, g_ref, 1.0)


def _proj_res(x, a, w, g_post):
    n, k = a.shape
    tm = min(ROW_TM, n)
    vmem = (2 * 2 * tm * D * 4 + 2 * tm * k * 2 + 2 * k * D * 2 + tm * D * 4) // MIB + 6
    return pl.pallas_call(
        _proj_res_kernel,
        grid=(n // tm,),
        in_specs=[
            pl.BlockSpec((tm, D), lambda i: (i, 0)),
            pl.BlockSpec((tm, k), lambda i: (i, 0)),
            pl.BlockSpec((k, D), lambda i: (0, 0)),
            pl.BlockSpec((1, D), lambda i: (0, 0)),
        ],
        out_specs=pl.BlockSpec((tm, D), lambda i: (i, 0)),
        out_shape=jax.ShapeDtypeStruct((n, D), F32),
        scratch_shapes=[pltpu.VMEM((tm, D), F32)],
        compiler_params=_params(("parallel",), vmem),
        name="proj_res",
    )(x, a, w, g_post)


def _gelu(y):
    return 0.5 * y * (1.0 + lax.erf(y * (0.5 ** 0.5)))


def _gmlp_in_kernel(x_ref, g_ref, w_ref, lg_ref, lb_ref, o_ref, h_ref, y_ref):
    j = pl.program_id(0)
    _norm_rows_to(h_ref, x_ref, g_ref)

    def gelu_block(c):
        cs = slice(c * COL_BLOCK, (c + 1) * COL_BLOCK)
        return cs, _gelu(jnp.dot(h_ref[...], w_ref[:, cs], preferred_element_type=F32))

    @pl.when(j == 0)
    def _():
        for c in range(D // COL_BLOCK):
            cs, y = gelu_block(c)
            o_ref[:, cs] = y.astype(BF16)

    @pl.when(j == 1)
    def _():
        for c in range(D // COL_BLOCK):
            cs, y = gelu_block(c)
            y_ref[:, cs] = y

        def fn(rs):
            v = y_ref[rs, :]
            vc = v - jnp.mean(v, axis=-1, keepdims=True)
            var = jnp.mean(vc * vc, axis=-1, keepdims=True)
            o_ref[rs, :] = (vc * lax.rsqrt(var + LN_EPS) * lg_ref[...] + lb_ref[...]).astype(BF16)
        _row_chunks(x_ref.shape[0], fn)


def _gmlp_in(x, g_pre, w_in, ln_g, ln_b):
    n = x.shape[0]
    tm = min(ROW_TM, n)
    vmem = (2 * tm * D * 4 + tm * D * 2 + 2 * D * D * 2 + 2 * tm * D * 2 + tm * D * 4) // MIB + 6
    return pl.pallas_call(
        _gmlp_in_kernel,
        grid=(2, n // tm),
        in_specs=[
            pl.BlockSpec((tm, D), lambda j, i: (i, 0)),
            pl.BlockSpec((1, D), lambda j, i: (0, 0)),
            pl.BlockSpec((D, D), lambda j, i: (0, j)),
            pl.BlockSpec((1, D), lambda j, i: (0, 0)),
            pl.BlockSpec((1, D), lambda j, i: (0, 0)),
        ],
        out_specs=pl.BlockSpec((None, tm, D), lambda j, i: (j, i, 0)),
        out_shape=jax.ShapeDtypeStruct((2, n, D), BF16),
        scratch_shapes=[pltpu.VMEM((tm, D), BF16), pltpu.VMEM((tm, D), F32)],
        compiler_params=_params(("parallel", "parallel"), vmem),
        name="gmlp_in",
    )(x, g_pre, w_in, ln_g, ln_b)


def _gmlp_out_kernel(x_ref, u_ref, v_ref, ws_ref, bs_ref, w_ref, g_ref, o_ref, a_ref, y_ref):
    def fn(rs):
        for g in range(A_GROUPS):
            cs = slice(g * A_GROUP_DIM, (g + 1) * A_GROUP_DIM)
            mixed = jnp.dot(ws_ref[g], v_ref[rs, cs], preferred_element_type=F32) + bs_ref[:, g:g + 1]
            a_ref[rs, cs] = (u_ref[rs, cs].astype(F32) * mixed).astype(BF16)
    _row_chunks(x_ref.shape[0], fn)
    y_ref[...] = jnp.dot(a_ref[...], w_ref[...], preferred_element_type=F32)
    _postnorm_residual_to(o_ref, x_ref, y_ref, g_ref, 1.0)


def _gmlp_out(x, uv, w_s, b_s_t, w_out, g_post):
    n = x.shape[0]
    tm = min(ROW_TM, n)
    assert ROW_CHUNK == CHUNK and tm % CHUNK == 0
    vmem = (2 * 2 * tm * D * 4 + 2 * 2 * tm * D * 2 + 2 * D * D * 2 + tm * D * (2 + 4)) // MIB + 6
    return pl.pallas_call(
        _gmlp_out_kernel,
        grid=(n // tm,),
        in_specs=[
            pl.BlockSpec((tm, D), lambda i: (i, 0)),
            pl.BlockSpec((None, tm, D), lambda i: (0, i, 0)),
            pl.BlockSpec((None, tm, D), lambda i: (1, i, 0)),
            pl.BlockSpec((A_GROUPS, CHUNK, CHUNK), lambda i: (0, 0, 0)),
            pl.BlockSpec((CHUNK, A_GROUPS), lambda i: (0, 0)),
            pl.BlockSpec((D, D), lambda i: (0, 0)),
            pl.BlockSpec((1, D), lambda i: (0, 0)),
        ],
        out_specs=pl.BlockSpec((tm, D), lambda i: (i, 0)),
        out_shape=jax.ShapeDtypeStruct((n, D), F32),
        scratch_shapes=[pltpu.VMEM((tm, D), BF16), pltpu.VMEM((tm, D), F32)],
        compiler_params=_params(("parallel",), vmem),
        name="gmlp_out",
    )(x, uv, uv, w_s, b_s_t, w_out, g_post)


def _rotary_tables(s):
    half = ROT_DIM // 2
    inv_freq = 1.0 / (ROPE_THETA ** (jnp.arange(0, ROT_DIM, 2, dtype=F32) / ROT_DIM))
    ang = jnp.arange(s, dtype=F32)[:, None] * inv_freq[None, :]
    cos, sin = jnp.cos(ang), jnp.sin(ang)
    rest = B_HEAD_DIM - ROT_DIM
    ct = jnp.concatenate([cos, cos, jnp.ones((s, rest), F32)], axis=1)
    sa = jnp.concatenate([-sin, jnp.zeros((s, half + rest), F32)], axis=1)
    sb = jnp.concatenate([jnp.zeros((s, half), F32), sin, jnp.zeros((s, rest), F32)], axis=1)
    return ct, sa, sb


def _qkv_kernel(x_ref, g_ref, w_ref, ct_ref, sa_ref, sb_ref, qv_ref, kt_ref, h_ref, t_ref):
    j = pl.program_id(0)
    _norm_rows_to(h_ref, x_ref, g_ref)

    def block(c):
        return jnp.dot(h_ref[...], w_ref[:, c * COL_BLOCK:(c + 1) * COL_BLOCK], preferred_element_type=F32)

    def rotary_blocks(scale, write):
        t_ref[0] = ct_ref[...] * scale
        t_ref[1] = sa_ref[...] * scale
        t_ref[2] = sb_ref[...] * scale
        for c in range(D // COL_BLOCK):
            y = block(c)
            for k in range(COL_BLOCK // LANE):
                yc = y[:, k * LANE:(k + 1) * LANE]
                rot = (yc * t_ref[0] + pltpu.roll(yc, LANE - ROT_DIM // 2, 1) * t_ref[1]
                       + pltpu.roll(yc, ROT_DIM // 2, 1) * t_ref[2])
                write(c * COL_BLOCK + k * LANE, rot)

    @pl.when(j == 0)
    def _():
        def write(lo, rot):
            qv_ref[:, lo:lo + LANE] = rot.astype(BF16)
        rotary_blocks(Q_SCALE, write)

    @pl.when(j == 1)
    def _():
        def write(lo, rot):
            rt = rot.T.astype(BF16)
            tk = kt_ref.shape[2]
            for s in range(kt_ref.shape[0]):
                kt_ref[s, lo:lo + LANE, :] = rt[:, s * tk:(s + 1) * tk]
        rotary_blocks(1.0, write)

    @pl.when(j == 2)
    def _():
        for c in range(D // COL_BLOCK):
            qv_ref[:, c * COL_BLOCK:(c + 1) * COL_BLOCK] = block(c).astype(BF16)


def _att_tiles(seq):
    tq = min(ATT_TQ, seq, ATT_SCORE_MIB * MIB // (8 * seq))
    return tq, min(ATT_TILE // tq, seq, ROW_TM)


def _qkv(x, g_pre, w_qkv, tables, batch, seq):
    n = x.shape[0]
    tm = min(ROW_TM, seq)
    tk = _att_tiles(seq)[1]
    sub = tm // tk
    pos_tiles = seq // tm
    tiles = n // tm
    ct, sa, sb = tables
    vmem = (2 * tm * D * 4 + tm * D * 2 + 2 * D * D * 2 + 4 * tm * D * 2 + tm * COL_BLOCK * 4) // MIB + 8
    tab = pl.BlockSpec((tm, LANE), lambda j, i: (i % pos_tiles, 0))

    def qv_block(j, i):
        return (j // 2, jnp.where(j == 1, tiles - 1, i), 0)

    def kt_block(j, i):
        t = jnp.where(j == 0, 0, jnp.where(j == 1, i, tiles - 1))
        return (t // pos_tiles, t % pos_tiles, 0, 0)

    return pl.pallas_call(
        _qkv_kernel,
        grid=(3, tiles),
        in_specs=[
            pl.BlockSpec((tm, D), lambda j, i: (i, 0)),
            pl.BlockSpec((1, D), lambda j, i: (0, 0)),
            pl.BlockSpec((D, D), lambda j, i: (0, j)),
            tab, tab, tab,
        ],
        out_specs=[pl.BlockSpec((None, tm, D), qv_block),
                   pl.BlockSpec((None, sub, D, tk), kt_block)],
        out_shape=[jax.ShapeDtypeStruct((2, n, D), BF16),
                   jax.ShapeDtypeStruct((batch, seq // tk, D, tk), BF16)],
        scratch_shapes=[pltpu.VMEM((tm, D), BF16), pltpu.VMEM((3, tm, LANE), F32)],
        compiler_params=_params(("arbitrary", "arbitrary"), vmem),
        name="qkv",
    )(x, g_pre, w_qkv, ct, sa, sb)


def _diffattn_kernel(q_ref, kt_ref, v_ref, lq1_ref, lk1_ref, lq2_ref, lk2_ref, sg_ref, o_ref,
                     s_ref, m_ref, l_ref, p_ref, acc_ref, *, tk, lam_init):
    seq = v_ref.shape[0]
    lane_blocks = tk // LANE
    unroll = max(1, min(ATT_UNROLL, seq // tk // 2))
    lam = (jnp.exp(jnp.sum(lq1_ref[...] * lk1_ref[...], axis=-1, keepdims=True))
           - jnp.exp(jnp.sum(lq2_ref[...] * lk2_ref[...], axis=-1, keepdims=True)) + lam_init)
    m_ref[...] = jnp.full_like(m_ref, -jnp.inf)
    l_ref[...] = jnp.zeros_like(l_ref)
    acc_ref[...] = jnp.zeros_like(acc_ref)

    def scores(t, carry):
        for c in range(2):
            cs = slice(c * B_HEAD_DIM, (c + 1) * B_HEAD_DIM)
            s = jnp.dot(q_ref[:, cs], kt_ref[t, cs, :], preferred_element_type=F32)
            s_ref[c, t] = s
            m = m_ref[c]
            for j in range(lane_blocks):
                m = jnp.maximum(m, s[:, j * LANE:(j + 1) * LANE])
            m_ref[c] = m
        return carry

    lax.fori_loop(0, seq // tk, scores, 0, unroll=unroll)
    for c in range(2):
        m_ref[c] = jnp.broadcast_to(jnp.max(m_ref[c], axis=-1, keepdims=True), m_ref.shape[1:])

    def weights(t, carry):
        ks = pl.ds(pl.multiple_of(t * tk, tk), tk)
        for c in range(2):
            m = m_ref[c]
            l = l_ref[c]
            for j in range(lane_blocks):
                ls = slice(j * LANE, (j + 1) * LANE)
                p = jnp.exp2(s_ref[c, t, :, ls] - m)
                l = l + p
                p_ref[c, :, ls] = p.astype(BF16)
            l_ref[c] = l
            acc_ref[c] += jnp.dot(p_ref[c], v_ref[ks, :], preferred_element_type=F32)
        return carry

    lax.fori_loop(0, seq // tk, weights, 0, unroll=unroll)
    l0 = jnp.sum(l_ref[0], axis=-1, keepdims=True)
    l1 = jnp.sum(l_ref[1], axis=-1, keepdims=True)
    o = acc_ref[0] / l0 - lam * (acc_ref[1] / l1)
    o_ref[...] = (_rms(o, sg_ref[...], SUBLN_EPS) * (1.0 - lam_init)).astype(BF16)


def _diffattn(qv, kt, lam_q1, lam_k1, lam_q2, lam_k2, subln_g, batch, seq, layer_idx):
    n = batch * seq
    tq, tk = _att_tiles(seq)
    q_tiles = seq // tq
    hw = 2 * B_HEAD_DIM
    lam_init = 0.8 - 0.6 * math.exp(-0.3 * layer_idx)
    vec = pl.BlockSpec((1, B_HEAD_DIM), lambda b, h, i: (0, 0))
    vmem = (2 * 2 * seq * hw * 2 + 4 * tq * hw * 2 + 2 * tq * seq * 4 + 2 * tq * hw * 4 + 4 * tq * tk * 4) // MIB + 8
    return pl.pallas_call(
        functools.partial(_diffattn_kernel, tk=tk, lam_init=lam_init),
        grid=(batch, B_HEADS, q_tiles),
        in_specs=[
            pl.BlockSpec((None, tq, hw), lambda b, h, i: (0, b * q_tiles + i, h)),
            pl.BlockSpec((None, seq // tk, hw, tk), lambda b, h, i: (b, 0, h, 0)),
            pl.BlockSpec((None, seq, hw), lambda b, h, i: (1, b, h)),
            vec, vec, vec, vec,
            pl.BlockSpec((1, B_VALUE_DIM), lambda b, h, i: (0, 0)),
        ],
        out_specs=pl.BlockSpec((tq, hw), lambda b, h, i: (b * q_tiles + i, h)),
        out_shape=jax.ShapeDtypeStruct((n, D), BF16),
        scratch_shapes=[pltpu.VMEM((2, seq // tk, tq, tk), F32),
                        pltpu.VMEM((2, tq, LANE), F32), pltpu.VMEM((2, tq, LANE), F32),
                        pltpu.VMEM((2, tq, tk), BF16), pltpu.VMEM((2, tq, B_VALUE_DIM), F32)],
        compiler_params=_params(("parallel", "parallel", "arbitrary"), vmem),
        name="diffattn",
    )(qv, kt, qv, lam_q1, lam_k1, lam_q2, lam_k2, subln_g)


def _dft_tables(n):
    idx = jnp.arange(n, dtype=jnp.int32)
    ang = ((idx[:, None] * idx[None, :]) % n).astype(F32) * (2.0 * math.pi / n)
    return jnp.cos(ang), jnp.sin(ang)


def _fourier_chan_kernel(x_ref, g_ref, cs_ref, o_ref, h_ref):
    _norm_rows_to(h_ref, x_ref, g_ref)
    for g in range(C_GROUPS):
        cs = slice(g * C_GROUP_DIM, (g + 1) * C_GROUP_DIM)
        ab = jnp.dot(h_ref[:, cs], cs_ref[...], preferred_element_type=F32)
        o_ref[0, :, cs] = ab[:, :C_GROUP_DIM].astype(BF16)
        o_ref[1, :, cs] = ab[:, C_GROUP_DIM:].astype(BF16)


def _fourier_chan(x, g_pre, chan_cs, batch, seq):
    tm = min(ROW_TM, seq)
    tiles = seq // tm
    vmem = (2 * tm * D * 4 + tm * D * 2 + 2 * 2 * tm * D * 2) // MIB + 8
    return pl.pallas_call(
        _fourier_chan_kernel,
        grid=(batch, tiles),
        in_specs=[
            pl.BlockSpec((tm, D), lambda b, i: (b * tiles + i, 0)),
            pl.BlockSpec((1, D), lambda b, i: (0, 0)),
            pl.BlockSpec((C_GROUP_DIM, 2 * C_GROUP_DIM), lambda b, i: (0, 0)),
        ],
        out_specs=pl.BlockSpec((None, 2, tm, D), lambda b, i: (b, 0, i, 0)),
        out_shape=jax.ShapeDtypeStruct((batch, 2, seq, D), BF16),
        scratch_shapes=[pltpu.VMEM((tm, D), BF16)],
        compiler_params=_params(("parallel", "parallel"), vmem),
        name="fourier_chan",
    )(x, g_pre, chan_cs)


def _pos_factor_tables(seq):
    k = jnp.arange(seq, dtype=jnp.int32)

    def cos_sin(j):
        ang = ((j[:, None] * k[None, :]) % seq).astype(F32) * (2.0 * math.pi / seq)
        return jnp.cos(ang), jnp.sin(ang)

    ca, sa = cos_sin(jnp.arange(seq // LANE, dtype=jnp.int32) * LANE)
    cb, sb = cos_sin(jnp.arange(LANE, dtype=jnp.int32))
    return (jnp.concatenate([ca, -sa], axis=1), jnp.concatenate([-sa, -ca], axis=1),
            jnp.concatenate([cb, cb], axis=1), jnp.concatenate([sb, sb], axis=1))


def _fourier_pos_kernel(p_ref, q_ref, u_ref, v_ref, ab_ref, o_ref, t_ref, acc_ref, *, norm):
    k = pl.program_id(2)

    @pl.when(k == 0)
    def _():
        acc_ref[...] = jnp.zeros_like(acc_ref)

    for g in range(t_ref.shape[0] // LANE):
        t_ref[g * LANE:(g + 1) * LANE, :] = (p_ref[g:g + 1, :] * u_ref[...]
                                             + q_ref[g:g + 1, :] * v_ref[...]).astype(BF16)
    acc_ref[...] += jnp.dot(t_ref[...], ab_ref[...], preferred_element_type=F32)

    @pl.when(k == pl.num_programs(2) - 1)
    def _():
        def fn(rs):
            o_ref[rs, :] = (acc_ref[rs, :] * norm).astype(BF16)
        _row_chunks(acc_ref.shape[0], fn)


def _fourier_pos(tables, ab, batch, seq):
    tm = min(DFT_TM, seq)
    tk = min(DFT_TK, 2 * seq)
    tiles = seq // tm
    ab2 = ab.reshape(batch * 2 * seq, D)
    k_tiles = 2 * seq // tk
    groups = tm // LANE
    vmem = (tm * tk * 2 + 4 * LANE * tk * 4 + 2 * tk * D * 2 + tm * D * 4 + 2 * tm * D * 2) // MIB + 8
    coarse = pl.BlockSpec((groups, tk), lambda b, i, k: (i, k))
    fine = pl.BlockSpec((LANE, tk), lambda b, i, k: (0, k))
    return pl.pallas_call(
        functools.partial(_fourier_pos_kernel, norm=(seq * C_GROUP_DIM) ** -0.5),
        grid=(batch, tiles, k_tiles),
        in_specs=[coarse, coarse, fine, fine,
                  pl.BlockSpec((tk, D), lambda b, i, k: (b * k_tiles + k, 0))],
        out_specs=pl.BlockSpec((tm, D), lambda b, i, k: (b * tiles + i, 0)),
        out_shape=jax.ShapeDtypeStruct((batch * seq, D), BF16),
        scratch_shapes=[pltpu.VMEM((tm, tk), BF16), pltpu.VMEM((tm, D), F32)],
        compiler_params=_params(("parallel", "parallel", "arbitrary"), vmem),
        name="fourier_pos",
    )(*tables, ab2)


def _xattn_kernel(x_ref, gpre_ref, wq_ref, kv_ref, wo_ref, gpost_ref, o_ref, h_ref, q_ref, a_ref, y_ref):
    _norm_rows_to(h_ref, x_ref, gpre_ref)
    q_ref[...] = (jnp.dot(h_ref[...], wq_ref[...], preferred_element_type=F32)
                  * (X_HEAD_DIM ** -0.5)).astype(BF16)

    for h in range(X_HEADS):
        cs = slice(h * X_HEAD_DIM, (h + 1) * X_HEAD_DIM)
        vs = slice(X_WIDTH + h * X_HEAD_DIM, X_WIDTH + (h + 1) * X_HEAD_DIM)
        s = lax.dot_general(q_ref[:, cs], kv_ref[:, cs], (((1,), (1,)), ((), ())),
                            preferred_element_type=F32)
        e = jnp.exp(s - jnp.max(s, axis=-1, keepdims=True))
        pv = jnp.dot(e.astype(BF16), kv_ref[:, vs], preferred_element_type=F32)
        a_ref[:, cs] = (pv / jnp.sum(e, axis=-1, keepdims=True)).astype(BF16)
    y_ref[...] = jnp.dot(a_ref[...], wo_ref[...], preferred_element_type=F32)
    _postnorm_residual_to(o_ref, x_ref, y_ref, gpost_ref, 1.0)


def _xattn(x, g_pre, w_q, kv, w_out, g_post, seq):
    n = x.shape[0]
    tm = min(ROW_TM, seq)
    tiles = seq // tm
    vmem = (2 * 2 * tm * D * 4 + tm * D * (2 + 4) + 4 * D * X_WIDTH * 2 + 2 * tm * X_WIDTH * 2) // MIB + 8
    return pl.pallas_call(
        _xattn_kernel,
        grid=(n // tm,),
        in_specs=[
            pl.BlockSpec((tm, D), lambda i: (i, 0)),
            pl.BlockSpec((1, D), lambda i: (0, 0)),
            pl.BlockSpec((D, X_WIDTH), lambda i: (0, 0)),
            pl.BlockSpec((N_MEM, 2 * X_WIDTH), lambda i: (i // tiles, 0)),
            pl.BlockSpec((X_WIDTH, D), lambda i: (0, 0)),
            pl.BlockSpec((1, D), lambda i: (0, 0)),
        ],
        out_specs=pl.BlockSpec((tm, D), lambda i: (i, 0)),
        out_shape=jax.ShapeDtypeStruct((n, D), F32),
        scratch_shapes=[pltpu.VMEM((tm, D), BF16), pltpu.VMEM((tm, X_WIDTH), BF16),
                        pltpu.VMEM((tm, X_WIDTH), BF16), pltpu.VMEM((tm, D), F32)],
        compiler_params=_params(("parallel",), vmem),
        name="xattn",
    )(x, g_pre, w_q, kv, w_out, g_post)


def _prepare(p):
    pad = FF_PAD - D_FF
    w = dict(p)
    w["ffn_w_gate"] = jnp.pad(p["ffn_w_gate"].astype(BF16), ((0, 0), (0, 0), (0, 0), (0, pad)))
    w["ffn_w_up"] = jnp.pad(p["ffn_w_up"].astype(BF16), ((0, 0), (0, 0), (0, 0), (0, pad)))
    w["ffn_w_down"] = jnp.pad(p["ffn_w_down"].astype(BF16), ((0, 0), (0, 0), (0, pad), (0, 0)))
    for name in ("a_w_in", "a_w_s", "a_w_out", "b_w_qkv", "b_w_out", "c_w_out", "x_w_q", "x_w_kv", "x_w_out"):
        w[name] = p[name].astype(BF16)
    w["a_b_s_t"] = jnp.swapaxes(p["a_b_s"], 1, 2)
    cc, sc = _dft_tables(C_GROUP_DIM)
    w["chan_cs"] = jnp.concatenate([cc, sc], axis=1).astype(BF16)
    return w


def _row(v):
    return v.reshape(1, -1)


def _trunk(x3, mem3, w, depth):
    batch, seq, _ = x3.shape
    x = x3.reshape(batch * seq, D)
    mem = mem3.reshape(batch * N_MEM, D)
    n_mixers = 3
    rot = pos_table = None
    for i in range(depth):
        kind, slot = i % n_mixers, i // n_mixers
        pre, post = w["ln_pre"][i], w["ln_post"][i]
        x = _ffn(x, _row(pre[0]), w["ffn_w_gate"][i, 0], w["ffn_w_up"][i, 0], w["ffn_w_down"][i, 0], _row(post[0]))
        if kind == 0:
            uv = _gmlp_in(x, _row(pre[1]), w["a_w_in"][slot], _row(w["a_ln_g"][slot]), _row(w["a_ln_b"][slot]))
            x = _gmlp_out(x, uv, w["a_w_s"][slot], w["a_b_s_t"][slot], w["a_w_out"][slot], _row(post[1]))
        elif kind == 1:
            if rot is None:
                rot = _rotary_tables(seq)
            qv, kt = _qkv(x, _row(pre[1]), w["b_w_qkv"][slot], rot, batch, seq)
            o = _diffattn(qv, kt, _row(w["b_lam_q1"][slot]), _row(w["b_lam_k1"][slot]), _row(w["b_lam_q2"][slot]),
                          _row(w["b_lam_k2"][slot]), _row(w["b_subln"][slot]), batch, seq, i)
            x = _proj_res(x, o, w["b_w_out"][slot], _row(post[1]))
        else:
            if pos_table is None:
                pos_table = _pos_factor_tables(seq)
            ab = _fourier_chan(x, _row(pre[1]), w["chan_cs"], batch, seq)
            mixed = _fourier_pos(pos_table, ab, batch, seq)
            x = _proj_res(x, mixed, w["c_w_out"][slot], _row(post[1]))
        kv = _norm_mm(mem, _row(w["ln_mem"][i]), w["x_w_kv"][i])
        x = _xattn(x, _row(pre[2]), w["x_w_q"][i], kv, w["x_w_out"][i], _row(post[2]), seq)
        x = _ffn(x, _row(pre[3]), w["ffn_w_gate"][i, 1], w["ffn_w_up"][i, 1], w["ffn_w_down"][i, 1], _row(post[3]))
    return x.reshape(batch, seq, D)


def kernel(x_prompt, x_sample, mem_prompt, mem_sample, ln_pre, ln_post, ln_mem, ffn_w_gate, ffn_w_up, ffn_w_down, a_w_in, a_ln_g, a_ln_b, a_w_s, a_b_s, a_w_out, b_w_qkv, b_lam_q1, b_lam_k1, b_lam_q2, b_lam_k2, b_subln, b_w_out, c_w_out, x_w_q, x_w_kv, x_w_out):
    params = {
        "ln_pre": ln_pre, "ln_post": ln_post, "ln_mem": ln_mem,
        "ffn_w_gate": ffn_w_gate, "ffn_w_up": ffn_w_up, "ffn_w_down": ffn_w_down,
        "a_w_in": a_w_in, "a_ln_g": a_ln_g, "a_ln_b": a_ln_b, "a_w_s": a_w_s, "a_b_s": a_b_s, "a_w_out": a_w_out,
        "b_w_qkv": b_w_qkv, "b_lam_q1": b_lam_q1, "b_lam_k1": b_lam_k1, "b_lam_q2": b_lam_q2,
        "b_lam_k2": b_lam_k2, "b_subln": b_subln, "b_w_out": b_w_out,
        "c_w_out": c_w_out, "x_w_q": x_w_q, "x_w_kv": x_w_kv, "x_w_out": x_w_out,
    }
    depth = ln_pre.shape[0]
    w = _prepare(params)
    return (_trunk(x_prompt, mem_prompt, w, depth), _trunk(x_sample, mem_sample, w, depth))
```

```python
import functools
import math

import jax
import jax.numpy as jnp
from jax import lax
from jax.experimental import pallas as pl
from jax.experimental.pallas import tpu as pltpu

F32 = jnp.float32
BF16 = jnp.bfloat16

D = 2048
D_FF = 5504
LANE = 128
FF_PAD = 5632
CHUNK = 128
A_GROUPS = 8
A_GROUP_DIM = D // A_GROUPS
B_HEADS = 8
B_HEAD_DIM = 128
B_VALUE_DIM = 256
ROT_DIM = 32
ROPE_THETA = 500000.0
C_GROUPS = 8
C_GROUP_DIM = D // C_GROUPS
X_HEADS = 4
X_HEAD_DIM = 128
X_WIDTH = X_HEADS * X_HEAD_DIM
N_MEM = 256
NORM_EPS = 1e-6
SUBLN_EPS = 1e-5
LN_EPS = 1e-5
MIB = 1024 * 1024
VMEM_CAP_MIB = 60

FFN_TM, FFN_TF = 1024, 512
FFN_PIECES = 8
ROW_TM = 512
ROW_CHUNK = 128
NORM_CHUNK = ROW_TM
COL_BLOCK = 512
ATT_TQ = 512
ATT_SCORE_MIB = 16
ATT_TILE = 128 * 1024
ATT_UNROLL = 8
Q_SCALE = B_HEAD_DIM ** -0.5 * math.log2(math.e)
DFT_TM, DFT_TK = 1024, 2048


def _params(semantics, vmem_mib, flags=None):
    return pltpu.CompilerParams(dimension_semantics=semantics, flags=flags,
                                vmem_limit_bytes=min(vmem_mib, VMEM_CAP_MIB) * MIB)


def _row_chunks(n_rows, fn, rc=None):
    rc = min(rc or ROW_CHUNK, n_rows)

    def body(c, carry):
        fn(pl.ds(pl.multiple_of(c * rc, rc), rc))
        return carry

    lax.fori_loop(0, n_rows // rc, body, 0)


def _rms(x, g, eps):
    return x * lax.rsqrt(jnp.mean(x * x, axis=-1, keepdims=True) + eps) * g


def _norm_rows_to(h_ref, x_ref, g_ref):
    def fn(rs):
        h_ref[rs, :] = _rms(x_ref[rs, :], g_ref[...], NORM_EPS).astype(BF16)
    _row_chunks(x_ref.shape[0], fn, NORM_CHUNK)


def _postnorm_residual_to(o_ref, x_ref, y_ref, g_ref, scale):
    def fn(rs):
        o_ref[rs, :] = x_ref[rs, :] + scale * _rms(y_ref[rs, :], g_ref[...], NORM_EPS)
    _row_chunks(x_ref.shape[0], fn, NORM_CHUNK)


def _ffn_kernel(xn_ref, xp_ref, gpre_ref, wg_ref, wu_ref, wd_ref, gpost_ref, o_ref, h_ref, acc_ref,
                *, n_tiles, pieces):
    i = pl.program_id(0)
    f = pl.program_id(1)
    side = i % 2
    main = 1 - side
    pr = xn_ref.shape[0]
    rows = pl.ds(pl.multiple_of(jnp.minimum(f, pieces - 1) * pr, pr), pr)

    def norm_in():
        h_ref[side, rows, :] = _rms(xn_ref[...], gpre_ref[...], NORM_EPS).astype(BF16)

    def finish():
        o_ref[...] = xp_ref[...] + 0.5 * _rms(acc_ref[side, rows, :], gpost_ref[...], NORM_EPS)

    def step(first, with_norms):
        h = h_ref[main]
        gate = jnp.dot(h, wg_ref[...], preferred_element_type=F32)
        up = jnp.dot(h, wu_ref[...], preferred_element_type=F32)
        a = (gate * jax.nn.sigmoid(gate) * up).astype(BF16)
        d = jnp.dot(a, wd_ref[...], preferred_element_type=F32)
        if first:
            acc_ref[main] = d
        else:
            acc_ref[main] += d
        if with_norms:
            norm_in()
            finish()

    has_piece = f < pieces

    @pl.when(jnp.logical_and(i == 0, has_piece))
    def _():
        norm_in()

        @pl.when(f == 0)
        def _():
            acc_ref[...] = jnp.zeros_like(acc_ref)

    in_main = jnp.logical_and(i >= 1, i <= n_tiles)

    @pl.when(jnp.logical_and(in_main, f == 0))
    def _():
        step(True, True)

    @pl.when(jnp.logical_and(in_main, jnp.logical_and(f > 0, has_piece)))
    def _():
        step(False, True)

    @pl.when(jnp.logical_and(in_main, f >= pieces))
    def _():
        step(False, False)

    @pl.when(jnp.logical_and(i == n_tiles + 1, has_piece))
    def _():
        finish()


def _ffn(x, g_pre, wg, wu, wd, g_post):
    n = x.shape[0]
    tm = min(FFN_TM, n)
    tf = FFN_TF
    fp = wg.shape[1]
    n_tiles, nf = n // tm, fp // tf
    pieces = FFN_PIECES
    pr = tm // pieces
    assert pieces <= nf and pr % 16 == 0

    def piece(tile, f):
        return (jnp.where(tile < 0, 0, jnp.minimum(tile, n_tiles - 1) * pieces + jnp.minimum(f, pieces - 1)), 0)

    def wcol(i, f):
        return jnp.where(jnp.logical_and(i >= 1, i <= n_tiles), f, 0)

    vmem = (3 * 2 * pr * D * 4 + 2 * tm * D * (2 + 4) + 2 * 3 * D * tf * 2 + 4 * tm * tf * 4) // MIB + 6
    return pl.pallas_call(
        functools.partial(_ffn_kernel, n_tiles=n_tiles, pieces=pieces),
        grid=(n_tiles + 2, nf),
        in_specs=[
            pl.BlockSpec((pr, D), lambda i, f: piece(i, f)),
            pl.BlockSpec((pr, D), lambda i, f: piece(i - 2, f)),
            pl.BlockSpec((1, D), lambda i, f: (0, 0)),
            pl.BlockSpec((D, tf), lambda i, f: (0, wcol(i, f))),
            pl.BlockSpec((D, tf), lambda i, f: (0, wcol(i, f))),
            pl.BlockSpec((tf, D), lambda i, f: (wcol(i, f), 0)),
            pl.BlockSpec((1, D), lambda i, f: (0, 0)),
        ],
        out_specs=pl.BlockSpec((pr, D), lambda i, f: piece(i - 2, f)),
        out_shape=jax.ShapeDtypeStruct((n, D), F32),
        scratch_shapes=[pltpu.VMEM((2, tm, D), BF16), pltpu.VMEM((2, tm, D), F32)],
        compiler_params=_params(("arbitrary", "arbitrary"), vmem),
        name="ffn",
    )(x, x, g_pre, wg, wu, wd, g_post)


def _norm_mm_kernel(x_ref, g_ref, w_ref, o_ref, h_ref):
    _norm_rows_to(h_ref, x_ref, g_ref)
    o_ref[...] = jnp.dot(h_ref[...], w_ref[...], preferred_element_type=F32).astype(o_ref.dtype)


def _norm_mm(x, g, w):
    n, n_out = x.shape[0], w.shape[1]
    tm = min(ROW_TM, n)
    tn = min(1024, n_out)
    vmem = (2 * tm * D * 4 + tm * D * 2 + 2 * D * tn * 2 + 2 * tm * tn * 2 + tm * tn * 4) // MIB + 6
    return pl.pallas_call(
        _norm_mm_kernel,
        grid=(n_out // tn, n // tm),
        in_specs=[
            pl.BlockSpec((tm, D), lambda j, i: (i, 0)),
            pl.BlockSpec((1, D), lambda j, i: (0, 0)),
            pl.BlockSpec((D, tn), lambda j, i: (0, j)),
        ],
        out_specs=pl.BlockSpec((tm, tn), lambda j, i: (i, j)),
        out_shape=jax.ShapeDtypeStruct((n, n_out), BF16),
        scratch_shapes=[pltpu.VMEM((tm, D), BF16)],
        compiler_params=_params(("parallel", "parallel"), vmem),
        name="norm_mm",
    )(x, g, w)


def _proj_res_kernel(x_ref, a_ref, w_ref, g_ref, o_ref, y_ref):
    y_ref[...] = jnp.dot(a_ref[...], w_ref[...], preferred_element_type=F32)
    _postnorm_residual_to(o_ref, x_ref, y_ref, g_ref, 1.0)


def _proj_res(x, a, w, g_post):
    n, k = a.shape
    tm = min(ROW_TM, n)
    vmem = (2 * 2 * tm * D * 4 + 2 * tm * k * 2 + 2 * k * D * 2 + tm * D * 4) // MIB + 6
    return pl.pallas_call(
        _proj_res_kernel,
        grid=(n // tm,),
        in_specs=[
            pl.BlockSpec((tm, D), lambda i: (i, 0)),
            pl.BlockSpec((tm, k), lambda i: (i, 0)),
            pl.BlockSpec((k, D), lambda i: (0, 0)),
            pl.BlockSpec((1, D), lambda i: (0, 0)),
        ],
        out_specs=pl.BlockSpec((tm, D), lambda i: (i, 0)),
        out_shape=jax.ShapeDtypeStruct((n, D), F32),
        scratch_shapes=[pltpu.VMEM((tm, D), F32)],
        compiler_params=_params(("parallel",), vmem),
        name="proj_res",
    )(x, a, w, g_post)


def _gelu(y):
    return 0.5 * y * (1.0 + lax.erf(y * (0.5 ** 0.5)))


def _gmlp_in_kernel(x_ref, g_ref, w_ref, lg_ref, lb_ref, o_ref, h_ref, y_ref):
    j = pl.program_id(0)
    _norm_rows_to(h_ref, x_ref, g_ref)

    def gelu_block(c):
        cs = slice(c * COL_BLOCK, (c + 1) * COL_BLOCK)
        return cs, _gelu(jnp.dot(h_ref[...], w_ref[:, cs], preferred_element_type=F32))

    @pl.when(j == 0)
    def _():
        for c in range(D // COL_BLOCK):
            cs, y = gelu_block(c)
            o_ref[:, cs] = y.astype(BF16)

    @pl.when(j == 1)
    def _():
        for c in range(D // COL_BLOCK):
            cs, y = gelu_block(c)
            y_ref[:, cs] = y

        def fn(rs):
            v = y_ref[rs, :]
            vc = v - jnp.mean(v, axis=-1, keepdims=True)
            var = jnp.mean(vc * vc, axis=-1, keepdims=True)
            o_ref[rs, :] = (vc * lax.rsqrt(var + LN_EPS) * lg_ref[...] + lb_ref[...]).astype(BF16)
        _row_chunks(x_ref.shape[0], fn, NORM_CHUNK)


def _gmlp_in(x, g_pre, w_in, ln_g, ln_b):
    n = x.shape[0]
    tm = min(ROW_TM, n)
    vmem = (2 * tm * D * 4 + tm * D * 2 + 2 * D * D * 2 + 2 * tm * D * 2 + tm * D * 4) // MIB + 6
    return pl.pallas_call(
        _gmlp_in_kernel,
        grid=(2, n // tm),
        in_specs=[
            pl.BlockSpec((tm, D), lambda j, i: (i, 0)),
            pl.BlockSpec((1, D), lambda j, i: (0, 0)),
            pl.BlockSpec((D, D), lambda j, i: (0, j)),
            pl.BlockSpec((1, D), lambda j, i: (0, 0)),
            pl.BlockSpec((1, D), lambda j, i: (0, 0)),
        ],
        out_specs=pl.BlockSpec((None, tm, D), lambda j, i: (j, i, 0)),
        out_shape=jax.ShapeDtypeStruct((2, n, D), BF16),
        scratch_shapes=[pltpu.VMEM((tm, D), BF16), pltpu.VMEM((tm, D), F32)],
        compiler_params=_params(("parallel", "parallel"), vmem),
        name="gmlp_in",
    )(x, g_pre, w_in, ln_g, ln_b)


def _gmlp_out_kernel(x_ref, u_ref, v_ref, ws_ref, bs_ref, w_ref, g_ref, o_ref, a_ref, y_ref):
    for c in range(x_ref.shape[0] // CHUNK):
        rs = slice(c * CHUNK, (c + 1) * CHUNK)
        for g in range(A_GROUPS):
            cs = slice(g * A_GROUP_DIM, (g + 1) * A_GROUP_DIM)
            mixed = jnp.dot(ws_ref[g], v_ref[rs, cs], preferred_element_type=F32) + bs_ref[:, g:g + 1]
            a_ref[rs, cs] = (u_ref[rs, cs].astype(F32) * mixed).astype(BF16)
    y_ref[...] = jnp.dot(a_ref[...], w_ref[...], preferred_element_type=F32)
    _postnorm_residual_to(o_ref, x_ref, y_ref, g_ref, 1.0)


def _gmlp_out(x, uv, w_s, b_s_t, w_out, g_post):
    n = x.shape[0]
    tm = min(ROW_TM, n)
    assert ROW_CHUNK == CHUNK and tm % CHUNK == 0
    vmem = (2 * 2 * tm * D * 4 + 2 * 2 * tm * D * 2 + 2 * D * D * 2 + tm * D * (2 + 4)) // MIB + 6
    return pl.pallas_call(
        _gmlp_out_kernel,
        grid=(n // tm,),
        in_specs=[
            pl.BlockSpec((tm, D), lambda i: (i, 0)),
            pl.BlockSpec((None, tm, D), lambda i: (0, i, 0)),
            pl.BlockSpec((None, tm, D), lambda i: (1, i, 0)),
            pl.BlockSpec((A_GROUPS, CHUNK, CHUNK), lambda i: (0, 0, 0)),
            pl.BlockSpec((CHUNK, A_GROUPS), lambda i: (0, 0)),
            pl.BlockSpec((D, D), lambda i: (0, 0)),
            pl.BlockSpec((1, D), lambda i: (0, 0)),
        ],
        out_specs=pl.BlockSpec((tm, D), lambda i: (i, 0)),
        out_shape=jax.ShapeDtypeStruct((n, D), F32),
        scratch_shapes=[pltpu.VMEM((tm, D), BF16), pltpu.VMEM((tm, D), F32)],
        compiler_params=_params(("parallel",), vmem),
        name="gmlp_out",
    )(x, uv, uv, w_s, b_s_t, w_out, g_post)


def _rotary_tables(s):
    half = ROT_DIM // 2
    inv_freq = 1.0 / (ROPE_THETA ** (jnp.arange(0, ROT_DIM, 2, dtype=F32) / ROT_DIM))
    ang = jnp.arange(s, dtype=F32)[:, None] * inv_freq[None, :]
    cos, sin = jnp.cos(ang), jnp.sin(ang)
    rest = B_HEAD_DIM - ROT_DIM
    ct = jnp.concatenate([cos, cos, jnp.ones((s, rest), F32)], axis=1)
    sa = jnp.concatenate([-sin, jnp.zeros((s, half + rest), F32)], axis=1)
    sb = jnp.concatenate([jnp.zeros((s, half), F32), sin, jnp.zeros((s, rest), F32)], axis=1)
    return ct, sa, sb


def _qkv_kernel(x_ref, g_ref, w_ref, ct_ref, sa_ref, sb_ref, qv_ref, kt_ref, h_ref, t_ref):
    j = pl.program_id(0)
    _norm_rows_to(h_ref, x_ref, g_ref)

    def block(c):
        return jnp.dot(h_ref[...], w_ref[:, c * COL_BLOCK:(c + 1) * COL_BLOCK], preferred_element_type=F32)

    def rotary_blocks(scale, write):
        t_ref[0] = ct_ref[...] * scale
        t_ref[1] = sa_ref[...] * scale
        t_ref[2] = sb_ref[...] * scale
        for c in range(D // COL_BLOCK):
            y = block(c)
            for k in range(COL_BLOCK // LANE):
                yc = y[:, k * LANE:(k + 1) * LANE]
                rot = (yc * t_ref[0] + pltpu.roll(yc, LANE - ROT_DIM // 2, 1) * t_ref[1]
                       + pltpu.roll(yc, ROT_DIM // 2, 1) * t_ref[2])
                write(c * COL_BLOCK + k * LANE, rot)

    @pl.when(j == 0)
    def _():
        def write(lo, rot):
            qv_ref[:, lo:lo + LANE] = rot.astype(BF16)
        rotary_blocks(Q_SCALE, write)

    @pl.when(j == 1)
    def _():
        def write(lo, rot):
            rt = rot.T.astype(BF16)
            tk = kt_ref.shape[2]
            for s in range(kt_ref.shape[0]):
                kt_ref[s, lo:lo + LANE, :] = rt[:, s * tk:(s + 1) * tk]
        rotary_blocks(1.0, write)

    @pl.when(j == 2)
    def _():
        for c in range(D // COL_BLOCK):
            qv_ref[:, c * COL_BLOCK:(c + 1) * COL_BLOCK] = block(c).astype(BF16)


def _att_tiles(seq):
    tq = min(ATT_TQ, seq, ATT_SCORE_MIB * MIB // (8 * seq))
    return tq, min(ATT_TILE // tq, seq, ROW_TM)


def _qkv(x, g_pre, w_qkv, tables, batch, seq):
    n = x.shape[0]
    tm = min(ROW_TM, seq)
    tk = _att_tiles(seq)[1]
    sub = tm // tk
    pos_tiles = seq // tm
    tiles = n // tm
    ct, sa, sb = tables
    vmem = (2 * tm * D * 4 + tm * D * 2 + 2 * D * D * 2 + 4 * tm * D * 2 + tm * COL_BLOCK * 4) // MIB + 8
    tab = pl.BlockSpec((tm, LANE), lambda j, i: (i % pos_tiles, 0))

    def qv_block(j, i):
        return (j // 2, jnp.where(j == 1, tiles - 1, i), 0)

    def kt_block(j, i):
        t = jnp.where(j == 0, 0, jnp.where(j == 1, i, tiles - 1))
        return (t // pos_tiles, t % pos_tiles, 0, 0)

    return pl.pallas_call(
        _qkv_kernel,
        grid=(3, tiles),
        in_specs=[
            pl.BlockSpec((tm, D), lambda j, i: (i, 0)),
            pl.BlockSpec((1, D), lambda j, i: (0, 0)),
            pl.BlockSpec((D, D), lambda j, i: (0, j)),
            tab, tab, tab,
        ],
        out_specs=[pl.BlockSpec((None, tm, D), qv_block),
                   pl.BlockSpec((None, sub, D, tk), kt_block)],
        out_shape=[jax.ShapeDtypeStruct((2, n, D), BF16),
                   jax.ShapeDtypeStruct((batch, seq // tk, D, tk), BF16)],
        scratch_shapes=[pltpu.VMEM((tm, D), BF16), pltpu.VMEM((3, tm, LANE), F32)],
        compiler_params=_params(("arbitrary", "arbitrary"), vmem),
        name="qkv",
    )(x, g_pre, w_qkv, ct, sa, sb)


def _diffattn_kernel(q_ref, kt_ref, v_ref, lq1_ref, lk1_ref, lq2_ref, lk2_ref, sg_ref, o_ref,
                     s_ref, m_ref, l_ref, p_ref, acc_ref, *, tk, lam_init):
    seq = v_ref.shape[0]
    lane_blocks = tk // LANE
    unroll = max(1, min(ATT_UNROLL, seq // tk // 2))
    lam = (jnp.exp(jnp.sum(lq1_ref[...] * lk1_ref[...], axis=-1, keepdims=True))
           - jnp.exp(jnp.sum(lq2_ref[...] * lk2_ref[...], axis=-1, keepdims=True)) + lam_init)
    m_ref[...] = jnp.full_like(m_ref, -jnp.inf)
    l_ref[...] = jnp.zeros_like(l_ref)
    acc_ref[...] = jnp.zeros_like(acc_ref)

    def scores(t, carry):
        for c in range(2):
            cs = slice(c * B_HEAD_DIM, (c + 1) * B_HEAD_DIM)
            s = jnp.dot(q_ref[:, cs], kt_ref[t, cs, :], preferred_element_type=F32)
            s_ref[c, t] = s
            m = m_ref[c]
            for j in range(lane_blocks):
                m = jnp.maximum(m, s[:, j * LANE:(j + 1) * LANE])
            m_ref[c] = m
        return carry

    lax.fori_loop(0, seq // tk, scores, 0, unroll=unroll)
    for c in range(2):
        m_ref[c] = jnp.broadcast_to(jnp.max(m_ref[c], axis=-1, keepdims=True), m_ref.shape[1:])

    def weights(t, carry):
        ks = pl.ds(pl.multiple_of(t * tk, tk), tk)
        for c in range(2):
            m = m_ref[c]
            l = l_ref[c]
            for j in range(lane_blocks):
                ls = slice(j * LANE, (j + 1) * LANE)
                p = jnp.exp2(s_ref[c, t, :, ls] - m)
                l = l + p
                p_ref[c, :, ls] = p.astype(BF16)
            l_ref[c] = l
            acc_ref[c] += jnp.dot(p_ref[c], v_ref[ks, :], preferred_element_type=F32)
        return carry

    lax.fori_loop(0, seq // tk, weights, 0, unroll=unroll)
    l0 = jnp.sum(l_ref[0], axis=-1, keepdims=True)
    l1 = jnp.sum(l_ref[1], axis=-1, keepdims=True)
    o = acc_ref[0] / l0 - lam * (acc_ref[1] / l1)
    o_ref[...] = (_rms(o, sg_ref[...], SUBLN_EPS) * (1.0 - lam_init)).astype(BF16)


def _diffattn(qv, kt, lam_q1, lam_k1, lam_q2, lam_k2, subln_g, batch, seq, layer_idx):
    n = batch * seq
    tq, tk = _att_tiles(seq)
    q_tiles = seq // tq
    hw = 2 * B_HEAD_DIM
    lam_init = 0.8 - 0.6 * math.exp(-0.3 * layer_idx)
    vec = pl.BlockSpec((1, B_HEAD_DIM), lambda b, h, i: (0, 0))
    vmem = (2 * 2 * seq * hw * 2 + 4 * tq * hw * 2 + 2 * tq * seq * 4 + 2 * tq * hw * 4 + 4 * tq * tk * 4) // MIB + 8
    return pl.pallas_call(
        functools.partial(_diffattn_kernel, tk=tk, lam_init=lam_init),
        grid=(batch, B_HEADS, q_tiles),
        in_specs=[
            pl.BlockSpec((None, tq, hw), lambda b, h, i: (0, b * q_tiles + i, h)),
            pl.BlockSpec((None, seq // tk, hw, tk), lambda b, h, i: (b, 0, h, 0)),
            pl.BlockSpec((None, seq, hw), lambda b, h, i: (1, b, h)),
            vec, vec, vec, vec,
            pl.BlockSpec((1, B_VALUE_DIM), lambda b, h, i: (0, 0)),
        ],
        out_specs=pl.BlockSpec((tq, hw), lambda b, h, i: (b * q_tiles + i, h)),
        out_shape=jax.ShapeDtypeStruct((n, D), BF16),
        scratch_shapes=[pltpu.VMEM((2, seq // tk, tq, tk), F32),
                        pltpu.VMEM((2, tq, LANE), F32), pltpu.VMEM((2, tq, LANE), F32),
                        pltpu.VMEM((2, tq, tk), BF16), pltpu.VMEM((2, tq, B_VALUE_DIM), F32)],
        compiler_params=_params(("parallel", "parallel", "arbitrary"), vmem),
        name="diffattn",
    )(qv, kt, qv, lam_q1, lam_k1, lam_q2, lam_k2, subln_g)


def _dft_tables(n):
    idx = jnp.arange(n, dtype=jnp.int32)
    ang = ((idx[:, None] * idx[None, :]) % n).astype(F32) * (2.0 * math.pi / n)
    return jnp.cos(ang), jnp.sin(ang)


def _fourier_chan_kernel(x_ref, g_ref, cs_ref, o_ref, h_ref):
    _norm_rows_to(h_ref, x_ref, g_ref)
    for g in range(C_GROUPS):
        cs = slice(g * C_GROUP_DIM, (g + 1) * C_GROUP_DIM)
        ab = jnp.dot(h_ref[:, cs], cs_ref[...], preferred_element_type=F32)
        o_ref[0, :, cs] = ab[:, :C_GROUP_DIM].astype(BF16)
        o_ref[1, :, cs] = ab[:, C_GROUP_DIM:].astype(BF16)


def _fourier_chan(x, g_pre, chan_cs, batch, seq):
    tm = min(ROW_TM, seq)
    tiles = seq // tm
    vmem = (2 * tm * D * 4 + tm * D * 2 + 2 * 2 * tm * D * 2) // MIB + 8
    return pl.pallas_call(
        _fourier_chan_kernel,
        grid=(batch, tiles),
        in_specs=[
            pl.BlockSpec((tm, D), lambda b, i: (b * tiles + i, 0)),
            pl.BlockSpec((1, D), lambda b, i: (0, 0)),
            pl.BlockSpec((C_GROUP_DIM, 2 * C_GROUP_DIM), lambda b, i: (0, 0)),
        ],
        out_specs=pl.BlockSpec((None, 2, tm, D), lambda b, i: (b, 0, i, 0)),
        out_shape=jax.ShapeDtypeStruct((batch, 2, seq, D), BF16),
        scratch_shapes=[pltpu.VMEM((tm, D), BF16)],
        compiler_params=_params(("parallel", "parallel"), vmem),
        name="fourier_chan",
    )(x, g_pre, chan_cs)


def _pos_factor_tables(seq):
    k = jnp.arange(seq, dtype=jnp.int32)

    def cos_sin(j):
        ang = ((j[:, None] * k[None, :]) % seq).astype(F32) * (2.0 * math.pi / seq)
        return jnp.cos(ang), jnp.sin(ang)

    ca, sa = cos_sin(jnp.arange(seq // LANE, dtype=jnp.int32) * LANE)
    cb, sb = cos_sin(jnp.arange(LANE, dtype=jnp.int32))
    return (jnp.concatenate([ca, -sa], axis=1), jnp.concatenate([-sa, -ca], axis=1),
            jnp.concatenate([cb, cb], axis=1), jnp.concatenate([sb, sb], axis=1))


def _fourier_pos_kernel(p_ref, q_ref, u_ref, v_ref, ab_ref, o_ref, t_ref, acc_ref, *, norm):
    k = pl.program_id(2)

    @pl.when(k == 0)
    def _():
        acc_ref[...] = jnp.zeros_like(acc_ref)

    for g in range(t_ref.shape[0] // LANE):
        t_ref[g * LANE:(g + 1) * LANE, :] = (p_ref[g:g + 1, :] * u_ref[...]
                                             + q_ref[g:g + 1, :] * v_ref[...]).astype(BF16)
    acc_ref[...] += jnp.dot(t_ref[...], ab_ref[...], preferred_element_type=F32)

    @pl.when(k == pl.num_programs(2) - 1)
    def _():
        def fn(rs):
            o_ref[rs, :] = (acc_ref[rs, :] * norm).astype(BF16)
        _row_chunks(acc_ref.shape[0], fn)


def _fourier_pos(tables, ab, batch, seq):
    tm = min(DFT_TM, seq)
    tk = min(DFT_TK, 2 * seq)
    tiles = seq // tm
    ab2 = ab.reshape(batch * 2 * seq, D)
    k_tiles = 2 * seq // tk
    groups = tm // LANE
    vmem = (tm * tk * 2 + 4 * LANE * tk * 4 + 2 * tk * D * 2 + tm * D * 4 + 2 * tm * D * 2) // MIB + 8
    coarse = pl.BlockSpec((groups, tk), lambda b, i, k: (i, k))
    fine = pl.BlockSpec((LANE, tk), lambda b, i, k: (0, k))
    return pl.pallas_call(
        functools.partial(_fourier_pos_kernel, norm=(seq * C_GROUP_DIM) ** -0.5),
        grid=(batch, tiles, k_tiles),
        in_specs=[coarse, coarse, fine, fine,
                  pl.BlockSpec((tk, D), lambda b, i, k: (b * k_tiles + k, 0))],
        out_specs=pl.BlockSpec((tm, D), lambda b, i, k: (b * tiles + i, 0)),
        out_shape=jax.ShapeDtypeStruct((batch * seq, D), BF16),
        scratch_shapes=[pltpu.VMEM((tm, tk), BF16), pltpu.VMEM((tm, D), F32)],
        compiler_params=_params(("parallel", "parallel", "arbitrary"), vmem),
        name="fourier_pos",
    )(*tables, ab2)


def _xattn_kernel(x_ref, gpre_ref, wq_ref, kv_ref, wo_ref, gpost_ref, o_ref, h_ref, q_ref, a_ref, y_ref):
    _norm_rows_to(h_ref, x_ref, gpre_ref)
    q_ref[...] = (jnp.dot(h_ref[...], wq_ref[...], preferred_element_type=F32)
                  * (X_HEAD_DIM ** -0.5)).astype(BF16)

    for h in range(X_HEADS):
        cs = slice(h * X_HEAD_DIM, (h + 1) * X_HEAD_DIM)
        vs = slice(X_WIDTH + h * X_HEAD_DIM, X_WIDTH + (h + 1) * X_HEAD_DIM)
        s = lax.dot_general(q_ref[:, cs], kv_ref[:, cs], (((1,), (1,)), ((), ())),
                            preferred_element_type=F32)
        e = jnp.exp(s - jnp.max(s, axis=-1, keepdims=True))
        pv = jnp.dot(e.astype(BF16), kv_ref[:, vs], preferred_element_type=F32)
        a_ref[:, cs] = (pv / jnp.sum(e, axis=-1, keepdims=True)).astype(BF16)
    y_ref[...] = jnp.dot(a_ref[...], wo_ref[...], preferred_element_type=F32)
    _postnorm_residual_to(o_ref, x_ref, y_ref, gpost_ref, 1.0)


def _xattn(x, g_pre, w_q, kv, w_out, g_post, seq):
    n = x.shape[0]
    tm = min(ROW_TM, seq)
    tiles = seq // tm
    vmem = (2 * 2 * tm * D * 4 + tm * D * (2 + 4) + 4 * D * X_WIDTH * 2 + 2 * tm * X_WIDTH * 2) // MIB + 8
    return pl.pallas_call(
        _xattn_kernel,
        grid=(n // tm,),
        in_specs=[
            pl.BlockSpec((tm, D), lambda i: (i, 0)),
            pl.BlockSpec((1, D), lambda i: (0, 0)),
            pl.BlockSpec((D, X_WIDTH), lambda i: (0, 0)),
            pl.BlockSpec((N_MEM, 2 * X_WIDTH), lambda i: (i // tiles, 0)),
            pl.BlockSpec((X_WIDTH, D), lambda i: (0, 0)),
            pl.BlockSpec((1, D), lambda i: (0, 0)),
        ],
        out_specs=pl.BlockSpec((tm, D), lambda i: (i, 0)),
        out_shape=jax.ShapeDtypeStruct((n, D), F32),
        scratch_shapes=[pltpu.VMEM((tm, D), BF16), pltpu.VMEM((tm, X_WIDTH), BF16),
                        pltpu.VMEM((tm, X_WIDTH), BF16), pltpu.VMEM((tm, D), F32)],
        compiler_params=_params(("parallel",), vmem),
        name="xattn",
    )(x, g_pre, w_q, kv, w_out, g_post)


def _prepare(p):
    pad = FF_PAD - D_FF
    w = dict(p)
    w["ffn_w_gate"] = jnp.pad(p["ffn_w_gate"].astype(BF16), ((0, 0), (0, 0), (0, 0), (0, pad)))
    w["ffn_w_up"] = jnp.pad(p["ffn_w_up"].astype(BF16), ((0, 0), (0, 0), (0, 0), (0, pad)))
    w["ffn_w_down"] = jnp.pad(p["ffn_w_down"].astype(BF16), ((0, 0), (0, 0), (0, pad), (0, 0)))
    for name in ("a_w_in", "a_w_s", "a_w_out", "b_w_qkv", "b_w_out", "c_w_out", "x_w_q", "x_w_kv", "x_w_out"):
        w[name] = p[name].astype(BF16)
    w["a_b_s_t"] = jnp.swapaxes(p["a_b_s"], 1, 2)
    cc, sc = _dft_tables(C_GROUP_DIM)
    w["chan_cs"] = jnp.concatenate([cc, sc], axis=1).astype(BF16)
    return w


def _row(v):
    return v.reshape(1, -1)


def _trunk(x3, mem3, w, depth):
    batch, seq, _ = x3.shape
    x = x3.reshape(batch * seq, D)
    mem = mem3.reshape(batch * N_MEM, D)
    n_mixers = 3
    rot = pos_table = None
    for i in range(depth):
        kind, slot = i % n_mixers, i // n_mixers
        pre, post = w["ln_pre"][i], w["ln_post"][i]
        x = _ffn(x, _row(pre[0]), w["ffn_w_gate"][i, 0], w["ffn_w_up"][i, 0], w["ffn_w_down"][i, 0], _row(post[0]))
        if kind == 0:
            uv = _gmlp_in(x, _row(pre[1]), w["a_w_in"][slot], _row(w["a_ln_g"][slot]), _row(w["a_ln_b"][slot]))
            x = _gmlp_out(x, uv, w["a_w_s"][slot], w["a_b_s_t"][slot], w["a_w_out"][slot], _row(post[1]))
        elif kind == 1:
            if rot is None:
                rot = _rotary_tables(seq)
            qv, kt = _qkv(x, _row(pre[1]), w["b_w_qkv"][slot], rot, batch, seq)
            o = _diffattn(qv, kt, _row(w["b_lam_q1"][slot]), _row(w["b_lam_k1"][slot]), _row(w["b_lam_q2"][slot]),
                          _row(w["b_lam_k2"][slot]), _row(w["b_subln"][slot]), batch, seq, i)
            x = _proj_res(x, o, w["b_w_out"][slot], _row(post[1]))
        else:
            if pos_table is None:
                pos_table = _pos_factor_tables(seq)
            ab = _fourier_chan(x, _row(pre[1]), w["chan_cs"], batch, seq)
            mixed = _fourier_pos(pos_table, ab, batch, seq)
            x = _proj_res(x, mixed, w["c_w_out"][slot], _row(post[1]))
        kv = _norm_mm(mem, _row(w["ln_mem"][i]), w["x_w_kv"][i])
        x = _xattn(x, _row(pre[2]), w["x_w_q"][i], kv, w["x_w_out"][i], _row(post[2]), seq)
        x = _ffn(x, _row(pre[3]), w["ffn_w_gate"][i, 1], w["ffn_w_up"][i, 1], w["ffn_w_down"][i, 1], _row(post[3]))
    return x.reshape(batch, seq, D)


def kernel(x_prompt, x_sample, mem_prompt, mem_sample, ln_pre, ln_post, ln_mem, ffn_w_gate, ffn_w_up, ffn_w_down, a_w_in, a_ln_g, a_ln_b, a_w_s, a_b_s, a_w_out, b_w_qkv, b_lam_q1, b_lam_k1, b_lam_q2, b_lam_k2, b_subln, b_w_out, c_w_out, x_w_q, x_w_kv, x_w_out):
    params = {
        "ln_pre": ln_pre, "ln_post": ln_post, "ln_mem": ln_mem,
        "ffn_w_gate": ffn_w_gate, "ffn_w_up": ffn_w_up, "ffn_w_down": ffn_w_down,
        "a_w_in": a_w_in, "a_ln_g": a_ln_g, "a_ln_b": a_ln_b, "a_w_s": a_w_s, "a_b_s": a_b_s, "a_w_out": a_w_out,
        "b_w_qkv": b_w_qkv, "b_lam_q1": b_lam_q1, "b_lam_k1": b_lam_k1, "b_lam_q2": b_lam_q2,
        "b_lam_k2": b_lam_k2, "b_subln": b_subln, "b_w_out": b_w_out,
        "c_w_out": c_w_out, "x_w_q": x_w_q, "x_w_kv": x_w_kv, "x_w_out": x_w_out,
    }
    depth = ln_pre.shape[0]
    w = _prepare(params)
    return (_trunk(x_prompt, mem_prompt, w, depth), _trunk(x_sample, mem_sample, w, depth))
```

```python
import functools
import math

import jax
import jax.numpy as jnp
from jax import lax
from jax.experimental import pallas as pl
from jax.experimental.pallas import tpu as pltpu

F32 = jnp.float32
BF16 = jnp.bfloat16

D = 2048
D_FF = 5504
LANE = 128
FF_PAD = 5632
CHUNK = 128
A_GROUPS = 8
A_GROUP_DIM = D // A_GROUPS
B_HEADS = 8
B_HEAD_DIM = 128
B_VALUE_DIM = 256
ROT_DIM = 32
ROPE_THETA = 500000.0
C_GROUPS = 8
C_GROUP_DIM = D // C_GROUPS
X_HEADS = 4
X_HEAD_DIM = 128
X_WIDTH = X_HEADS * X_HEAD_DIM
N_MEM = 256
NORM_EPS = 1e-6
SUBLN_EPS = 1e-5
LN_EPS = 1e-5
MIB = 1024 * 1024
VMEM_CAP_MIB = 60

FFN_TM, FFN_TF = 1024, 512
FFN_PIECES = 8
ROW_TM = 512
ROW_CHUNK = 128
NORM_CHUNK = ROW_TM
COL_BLOCK = 512
ATT_TQ = 512
ATT_SCORE_MIB = 16
ATT_TILE = 128 * 1024
ATT_UNROLL = 8
Q_SCALE = B_HEAD_DIM ** -0.5 * math.log2(math.e)
DFT_TM, DFT_TK = 512, 2048


def _params(semantics, vmem_mib, flags=None):
    return pltpu.CompilerParams(dimension_semantics=semantics, flags=flags,
                                vmem_limit_bytes=min(vmem_mib, VMEM_CAP_MIB) * MIB)


def _row_chunks(n_rows, fn, rc=None):
    rc = min(rc or ROW_CHUNK, n_rows)

    def body(c, carry):
        fn(pl.ds(pl.multiple_of(c * rc, rc), rc))
        return carry

    lax.fori_loop(0, n_rows // rc, body, 0)


def _rms(x, g, eps):
    return x * lax.rsqrt(jnp.mean(x * x, axis=-1, keepdims=True) + eps) * g


def _norm_rows_to(h_ref, x_ref, g_ref):
    def fn(rs):
        h_ref[rs, :] = _rms(x_ref[rs, :], g_ref[...], NORM_EPS).astype(BF16)
    _row_chunks(x_ref.shape[0], fn, NORM_CHUNK)


def _postnorm_residual_to(o_ref, x_ref, y_ref, g_ref, scale):
    def fn(rs):
        o_ref[rs, :] = x_ref[rs, :] + scale * _rms(y_ref[rs, :], g_ref[...], NORM_EPS)
    _row_chunks(x_ref.shape[0], fn, NORM_CHUNK)


def _ffn_kernel(xn_ref, xp_ref, gpre_ref, wg_ref, wu_ref, wd_ref, gpost_ref, o_ref, h_ref, acc_ref,
                *, n_tiles, pieces):
    i = pl.program_id(0)
    f = pl.program_id(1)
    side = i % 2
    main = 1 - side
    pr = xn_ref.shape[0]
    rows = pl.ds(pl.multiple_of(jnp.minimum(f, pieces - 1) * pr, pr), pr)

    def norm_in():
        h_ref[side, rows, :] = _rms(xn_ref[...], gpre_ref[...], NORM_EPS).astype(BF16)

    def finish():
        o_ref[...] = xp_ref[...] + 0.5 * _rms(acc_ref[side, rows, :], gpost_ref[...], NORM_EPS)

    def step(first, with_norms):
        h = h_ref[main]
        gate = jnp.dot(h, wg_ref[...], preferred_element_type=F32)
        up = jnp.dot(h, wu_ref[...], preferred_element_type=F32)
        a = (gate * jax.nn.sigmoid(gate) * up).astype(BF16)
        d = jnp.dot(a, wd_ref[...], preferred_element_type=F32)
        if first:
            acc_ref[main] = d
        else:
            acc_ref[main] += d
        if with_norms:
            norm_in()
            finish()

    has_piece = f < pieces

    @pl.when(jnp.logical_and(i == 0, has_piece))
    def _():
        norm_in()

        @pl.when(f == 0)
        def _():
            acc_ref[...] = jnp.zeros_like(acc_ref)

    in_main = jnp.logical_and(i >= 1, i <= n_tiles)

    @pl.when(jnp.logical_and(in_main, f == 0))
    def _():
        step(True, True)

    @pl.when(jnp.logical_and(in_main, jnp.logical_and(f > 0, has_piece)))
    def _():
        step(False, True)

    @pl.when(jnp.logical_and(in_main, f >= pieces))
    def _():
        step(False, False)

    @pl.when(jnp.logical_and(i == n_tiles + 1, has_piece))
    def _():
        finish()


def _ffn(x, g_pre, wg, wu, wd, g_post):
    n = x.shape[0]
    tm = min(FFN_TM, n)
    tf = FFN_TF
    fp = wg.shape[1]
    n_tiles, nf = n // tm, fp // tf
    pieces = FFN_PIECES
    pr = tm // pieces
    assert pieces <= nf and pr % 16 == 0

    def piece(tile, f):
        return (jnp.where(tile < 0, 0, jnp.minimum(tile, n_tiles - 1) * pieces + jnp.minimum(f, pieces - 1)), 0)

    def wcol(i, f):
        return jnp.where(jnp.logical_and(i >= 1, i <= n_tiles), f, 0)

    vmem = (3 * 2 * pr * D * 4 + 2 * tm * D * (2 + 4) + 2 * 3 * D * tf * 2 + 4 * tm * tf * 4) // MIB + 6
    return pl.pallas_call(
        functools.partial(_ffn_kernel, n_tiles=n_tiles, pieces=pieces),
        grid=(n_tiles + 2, nf),
        in_specs=[
            pl.BlockSpec((pr, D), lambda i, f: piece(i, f)),
            pl.BlockSpec((pr, D), lambda i, f: piece(i - 2, f)),
            pl.BlockSpec((1, D), lambda i, f: (0, 0)),
            pl.BlockSpec((D, tf), lambda i, f: (0, wcol(i, f))),
            pl.BlockSpec((D, tf), lambda i, f: (0, wcol(i, f))),
            pl.BlockSpec((tf, D), lambda i, f: (wcol(i, f), 0)),
            pl.BlockSpec((1, D), lambda i, f: (0, 0)),
        ],
        out_specs=pl.BlockSpec((pr, D), lambda i, f: piece(i - 2, f)),
        out_shape=jax.ShapeDtypeStruct((n, D), F32),
        scratch_shapes=[pltpu.VMEM((2, tm, D), BF16), pltpu.VMEM((2, tm, D), F32)],
        compiler_params=_params(("arbitrary", "arbitrary"), vmem),
        name="ffn",
    )(x, x, g_pre, wg, wu, wd, g_post)


def _norm_mm_kernel(x_ref, g_ref, w_ref, o_ref, h_ref):
    _norm_rows_to(h_ref, x_ref, g_ref)
    o_ref[...] = jnp.dot(h_ref[...], w_ref[...], preferred_element_type=F32).astype(o_ref.dtype)


def _norm_mm(x, g, w):
    n, n_out = x.shape[0], w.shape[1]
    tm = min(ROW_TM, n)
    tn = min(1024, n_out)
    vmem = (2 * tm * D * 4 + tm * D * 2 + 2 * D * tn * 2 + 2 * tm * tn * 2 + tm * tn * 4) // MIB + 6
    return pl.pallas_call(
        _norm_mm_kernel,
        grid=(n_out // tn, n // tm),
        in_specs=[
            pl.BlockSpec((tm, D), lambda j, i: (i, 0)),
            pl.BlockSpec((1, D), lambda j, i: (0, 0)),
            pl.BlockSpec((D, tn), lambda j, i: (0, j)),
        ],
        out_specs=pl.BlockSpec((tm, tn), lambda j, i: (i, j)),
        out_shape=jax.ShapeDtypeStruct((n, n_out), BF16),
        scratch_shapes=[pltpu.VMEM((tm, D), BF16)],
        compiler_params=_params(("parallel", "parallel"), vmem),
        name="norm_mm",
    )(x, g, w)


def _proj_res_kernel(x_ref, a_ref, w_ref, g_ref, o_ref, y_ref):
    y_ref[...] = jnp.dot(a_ref[...], w_ref[...], preferred_element_type=F32)
    _postnorm_residual_to(o_ref, x_ref, y_ref, g_ref, 1.0)


def _proj_res(x, a, w, g_post):
    n, k = a.shape
    tm = min(ROW_TM, n)
    vmem = (2 * 2 * tm * D * 4 + 2 * tm * k * 2 + 2 * k * D * 2 + tm * D * 4) // MIB + 6
    return pl.pallas_call(
        _proj_res_kernel,
        grid=(n // tm,),
        in_specs=[
            pl.BlockSpec((tm, D), lambda i: (i, 0)),
            pl.BlockSpec((tm, k), lambda i: (i, 0)),
            pl.BlockSpec((k, D), lambda i: (0, 0)),
            pl.BlockSpec((1, D), lambda i: (0, 0)),
        ],
        out_specs=pl.BlockSpec((tm, D), lambda i: (i, 0)),
        out_shape=jax.ShapeDtypeStruct((n, D), F32),
        scratch_shapes=[pltpu.VMEM((tm, D), F32)],
        compiler_params=_params(("parallel",), vmem),
        name="proj_res",
    )(x, a, w, g_post)


def _gelu(y):
    return 0.5 * y * (1.0 + lax.erf(y * (0.5 ** 0.5)))


def _gmlp_in_kernel(x_ref, g_ref, w_ref, lg_ref, lb_ref, o_ref, h_ref, y_ref):
    j = pl.program_id(0)
    _norm_rows_to(h_ref, x_ref, g_ref)

    def gelu_block(c):
        cs = slice(c * COL_BLOCK, (c + 1) * COL_BLOCK)
        return cs, _gelu(jnp.dot(h_ref[...], w_ref[:, cs], preferred_element_type=F32))

    @pl.when(j == 0)
    def _():
        for c in range(D // COL_BLOCK):
            cs, y = gelu_block(c)
            o_ref[:, cs] = y.astype(BF16)

    @pl.when(j == 1)
    def _():
        for c in range(D // COL_BLOCK):
            cs, y = gelu_block(c)
            y_ref[:, cs] = y

        def fn(rs):
            v = y_ref[rs, :]
            vc = v - jnp.mean(v, axis=-1, keepdims=True)
            var = jnp.mean(vc * vc, axis=-1, keepdims=True)
            o_ref[rs, :] = (vc * lax.rsqrt(var + LN_EPS) * lg_ref[...] + lb_ref[...]).astype(BF16)
        _row_chunks(x_ref.shape[0], fn, NORM_CHUNK)


def _gmlp_in(x, g_pre, w_in, ln_g, ln_b):
    n = x.shape[0]
    tm = min(ROW_TM, n)
    vmem = (2 * tm * D * 4 + tm * D * 2 + 2 * D * D * 2 + 2 * tm * D * 2 + tm * D * 4) // MIB + 6
    return pl.pallas_call(
        _gmlp_in_kernel,
        grid=(2, n // tm),
        in_specs=[
            pl.BlockSpec((tm, D), lambda j, i: (i, 0)),
            pl.BlockSpec((1, D), lambda j, i: (0, 0)),
            pl.BlockSpec((D, D), lambda j, i: (0, j)),
            pl.BlockSpec((1, D), lambda j, i: (0, 0)),
            pl.BlockSpec((1, D), lambda j, i: (0, 0)),
        ],
        out_specs=pl.BlockSpec((None, tm, D), lambda j, i: (j, i, 0)),
        out_shape=jax.ShapeDtypeStruct((2, n, D), BF16),
        scratch_shapes=[pltpu.VMEM((tm, D), BF16), pltpu.VMEM((tm, D), F32)],
        compiler_params=_params(("parallel", "parallel"), vmem),
        name="gmlp_in",
    )(x, g_pre, w_in, ln_g, ln_b)


def _gmlp_out_kernel(x_ref, u_ref, v_ref, ws_ref, bs_ref, w_ref, g_ref, o_ref, a_ref, y_ref):
    for c in range(x_ref.shape[0] // CHUNK):
        rs = slice(c * CHUNK, (c + 1) * CHUNK)
        for g in range(A_GROUPS):
            cs = slice(g * A_GROUP_DIM, (g + 1) * A_GROUP_DIM)
            mixed = jnp.dot(ws_ref[g], v_ref[rs, cs], preferred_element_type=F32) + bs_ref[:, g:g + 1]
            a_ref[rs, cs] = (u_ref[rs, cs].astype(F32) * mixed).astype(BF16)
    y_ref[...] = jnp.dot(a_ref[...], w_ref[...], preferred_element_type=F32)
    _postnorm_residual_to(o_ref, x_ref, y_ref, g_ref, 1.0)


def _gmlp_out(x, uv, w_s, b_s_t, w_out, g_post):
    n = x.shape[0]
    tm = min(ROW_TM, n)
    assert ROW_CHUNK == CHUNK and tm % CHUNK == 0
    vmem = (2 * 2 * tm * D * 4 + 2 * 2 * tm * D * 2 + 2 * D * D * 2 + tm * D * (2 + 4)) // MIB + 6
    return pl.pallas_call(
        _gmlp_out_kernel,
        grid=(n // tm,),
        in_specs=[
            pl.BlockSpec((tm, D), lambda i: (i, 0)),
            pl.BlockSpec((None, tm, D), lambda i: (0, i, 0)),
            pl.BlockSpec((None, tm, D), lambda i: (1, i, 0)),
            pl.BlockSpec((A_GROUPS, CHUNK, CHUNK), lambda i: (0, 0, 0)),
            pl.BlockSpec((CHUNK, A_GROUPS), lambda i: (0, 0)),
            pl.BlockSpec((D, D), lambda i: (0, 0)),
            pl.BlockSpec((1, D), lambda i: (0, 0)),
        ],
        out_specs=pl.BlockSpec((tm, D), lambda i: (i, 0)),
        out_shape=jax.ShapeDtypeStruct((n, D), F32),
        scratch_shapes=[pltpu.VMEM((tm, D), BF16), pltpu.VMEM((tm, D), F32)],
        compiler_params=_params(("parallel",), vmem),
        name="gmlp_out",
    )(x, uv, uv, w_s, b_s_t, w_out, g_post)


def _rotary_tables(s):
    half = ROT_DIM // 2
    inv_freq = 1.0 / (ROPE_THETA ** (jnp.arange(0, ROT_DIM, 2, dtype=F32) / ROT_DIM))
    ang = jnp.arange(s, dtype=F32)[:, None] * inv_freq[None, :]
    cos, sin = jnp.cos(ang), jnp.sin(ang)
    rest = B_HEAD_DIM - ROT_DIM
    ct = jnp.concatenate([cos, cos, jnp.ones((s, rest), F32)], axis=1)
    sa = jnp.concatenate([-sin, jnp.zeros((s, half + rest), F32)], axis=1)
    sb = jnp.concatenate([jnp.zeros((s, half), F32), sin, jnp.zeros((s, rest), F32)], axis=1)
    return ct, sa, sb


def _qkv_kernel(x_ref, g_ref, w_ref, ct_ref, sa_ref, sb_ref, qv_ref, kt_ref, h_ref, t_ref):
    j = pl.program_id(0)
    _norm_rows_to(h_ref, x_ref, g_ref)

    def block(c):
        return jnp.dot(h_ref[...], w_ref[:, c * COL_BLOCK:(c + 1) * COL_BLOCK], preferred_element_type=F32)

    def rotary_blocks(scale, write):
        t_ref[0] = ct_ref[...] * scale
        t_ref[1] = sa_ref[...] * scale
        t_ref[2] = sb_ref[...] * scale
        for c in range(D // COL_BLOCK):
            y = block(c)
            for k in range(COL_BLOCK // LANE):
                yc = y[:, k * LANE:(k + 1) * LANE]
                rot = (yc * t_ref[0] + pltpu.roll(yc, LANE - ROT_DIM // 2, 1) * t_ref[1]
                       + pltpu.roll(yc, ROT_DIM // 2, 1) * t_ref[2])
                write(c * COL_BLOCK + k * LANE, rot)

    @pl.when(j == 0)
    def _():
        def write(lo, rot):
            qv_ref[:, lo:lo + LANE] = rot.astype(BF16)
        rotary_blocks(Q_SCALE, write)

    @pl.when(j == 1)
    def _():
        def write(lo, rot):
            rt = rot.T.astype(BF16)
            tk = kt_ref.shape[2]
            for s in range(kt_ref.shape[0]):
                kt_ref[s, lo:lo + LANE, :] = rt[:, s * tk:(s + 1) * tk]
        rotary_blocks(1.0, write)

    @pl.when(j == 2)
    def _():
        for c in range(D // COL_BLOCK):
            qv_ref[:, c * COL_BLOCK:(c + 1) * COL_BLOCK] = block(c).astype(BF16)


def _att_tiles(seq):
    tq = min(ATT_TQ, seq, ATT_SCORE_MIB * MIB // (8 * seq))
    return tq, min(ATT_TILE // tq, seq, ROW_TM)


def _qkv(x, g_pre, w_qkv, tables, batch, seq):
    n = x.shape[0]
    tm = min(ROW_TM, seq)
    tk = _att_tiles(seq)[1]
    sub = tm // tk
    pos_tiles = seq // tm
    tiles = n // tm
    ct, sa, sb = tables
    vmem = (2 * tm * D * 4 + tm * D * 2 + 2 * D * D * 2 + 4 * tm * D * 2 + tm * COL_BLOCK * 4) // MIB + 8
    tab = pl.BlockSpec((tm, LANE), lambda j, i: (i % pos_tiles, 0))

    def qv_block(j, i):
        return (j // 2, jnp.where(j == 1, tiles - 1, i), 0)

    def kt_block(j, i):
        t = jnp.where(j == 0, 0, jnp.where(j == 1, i, tiles - 1))
        return (t // pos_tiles, t % pos_tiles, 0, 0)

    return pl.pallas_call(
        _qkv_kernel,
        grid=(3, tiles),
        in_specs=[
            pl.BlockSpec((tm, D), lambda j, i: (i, 0)),
            pl.BlockSpec((1, D), lambda j, i: (0, 0)),
            pl.BlockSpec((D, D), lambda j, i: (0, j)),
            tab, tab, tab,
        ],
        out_specs=[pl.BlockSpec((None, tm, D), qv_block),
                   pl.BlockSpec((None, sub, D, tk), kt_block)],
        out_shape=[jax.ShapeDtypeStruct((2, n, D), BF16),
                   jax.ShapeDtypeStruct((batch, seq // tk, D, tk), BF16)],
        scratch_shapes=[pltpu.VMEM((tm, D), BF16), pltpu.VMEM((3, tm, LANE), F32)],
        compiler_params=_params(("arbitrary", "arbitrary"), vmem),
        name="qkv",
    )(x, g_pre, w_qkv, ct, sa, sb)


def _diffattn_kernel(q_ref, kt_ref, v_ref, lq1_ref, lk1_ref, lq2_ref, lk2_ref, sg_ref, o_ref,
                     s_ref, m_ref, l_ref, p_ref, acc_ref, *, tk, lam_init):
    seq = v_ref.shape[0]
    lane_blocks = tk // LANE
    unroll = max(1, min(ATT_UNROLL, seq // tk // 2))
    lam = (jnp.exp(jnp.sum(lq1_ref[...] * lk1_ref[...], axis=-1, keepdims=True))
           - jnp.exp(jnp.sum(lq2_ref[...] * lk2_ref[...], axis=-1, keepdims=True)) + lam_init)
    m_ref[...] = jnp.full_like(m_ref, -jnp.inf)
    l_ref[...] = jnp.zeros_like(l_ref)
    acc_ref[...] = jnp.zeros_like(acc_ref)

    def scores(t, carry):
        for c in range(2):
            cs = slice(c * B_HEAD_DIM, (c + 1) * B_HEAD_DIM)
            s = jnp.dot(q_ref[:, cs], kt_ref[t, cs, :], preferred_element_type=F32)
            s_ref[c, t] = s
            m = m_ref[c]
            for j in range(lane_blocks):
                m = jnp.maximum(m, s[:, j * LANE:(j + 1) * LANE])
            m_ref[c] = m
        return carry

    lax.fori_loop(0, seq // tk, scores, 0, unroll=unroll)
    for c in range(2):
        m_ref[c] = jnp.broadcast_to(jnp.max(m_ref[c], axis=-1, keepdims=True), m_ref.shape[1:])

    def weights(t, carry):
        ks = pl.ds(pl.multiple_of(t * tk, tk), tk)
        for c in range(2):
            m = m_ref[c]
            l = l_ref[c]
            for j in range(lane_blocks):
                ls = slice(j * LANE, (j + 1) * LANE)
                p = jnp.exp2(s_ref[c, t, :, ls] - m)
                l = l + p
                p_ref[c, :, ls] = p.astype(BF16)
            l_ref[c] = l
            acc_ref[c] += jnp.dot(p_ref[c], v_ref[ks, :], preferred_element_type=F32)
        return carry

    lax.fori_loop(0, seq // tk, weights, 0, unroll=unroll)
    l0 = jnp.sum(l_ref[0], axis=-1, keepdims=True)
    l1 = jnp.sum(l_ref[1], axis=-1, keepdims=True)
    o = acc_ref[0] / l0 - lam * (acc_ref[1] / l1)
    o_ref[...] = (_rms(o, sg_ref[...], SUBLN_EPS) * (1.0 - lam_init)).astype(BF16)


def _diffattn(qv, kt, lam_q1, lam_k1, lam_q2, lam_k2, subln_g, batch, seq, layer_idx):
    n = batch * seq
    tq, tk = _att_tiles(seq)
    q_tiles = seq // tq
    hw = 2 * B_HEAD_DIM
    lam_init = 0.8 - 0.6 * math.exp(-0.3 * layer_idx)
    vec = pl.BlockSpec((1, B_HEAD_DIM), lambda b, h, i: (0, 0))
    vmem = (2 * 2 * seq * hw * 2 + 4 * tq * hw * 2 + 2 * tq * seq * 4 + 2 * tq * hw * 4 + 4 * tq * tk * 4) // MIB + 8
    return pl.pallas_call(
        functools.partial(_diffattn_kernel, tk=tk, lam_init=lam_init),
        grid=(batch, B_HEADS, q_tiles),
        in_specs=[
            pl.BlockSpec((None, tq, hw), lambda b, h, i: (0, b * q_tiles + i, h)),
            pl.BlockSpec((None, seq // tk, hw, tk), lambda b, h, i: (b, 0, h, 0)),
            pl.BlockSpec((None, seq, hw), lambda b, h, i: (1, b, h)),
            vec, vec, vec, vec,
            pl.BlockSpec((1, B_VALUE_DIM), lambda b, h, i: (0, 0)),
        ],
        out_specs=pl.BlockSpec((tq, hw), lambda b, h, i: (b * q_tiles + i, h)),
        out_shape=jax.ShapeDtypeStruct((n, D), BF16),
        scratch_shapes=[pltpu.VMEM((2, seq // tk, tq, tk), F32),
                        pltpu.VMEM((2, tq, LANE), F32), pltpu.VMEM((2, tq, LANE), F32),
                        pltpu.VMEM((2, tq, tk), BF16), pltpu.VMEM((2, tq, B_VALUE_DIM), F32)],
        compiler_params=_params(("parallel", "parallel", "arbitrary"), vmem),
        name="diffattn",
    )(qv, kt, qv, lam_q1, lam_k1, lam_q2, lam_k2, subln_g)


def _dft_tables(n):
    idx = jnp.arange(n, dtype=jnp.int32)
    ang = ((idx[:, None] * idx[None, :]) % n).astype(F32) * (2.0 * math.pi / n)
    return jnp.cos(ang), jnp.sin(ang)


def _fourier_chan_kernel(x_ref, g_ref, cs_ref, o_ref, h_ref):
    _norm_rows_to(h_ref, x_ref, g_ref)
    for g in range(C_GROUPS):
        cs = slice(g * C_GROUP_DIM, (g + 1) * C_GROUP_DIM)
        ab = jnp.dot(h_ref[:, cs], cs_ref[...], preferred_element_type=F32)
        o_ref[0, :, cs] = ab[:, :C_GROUP_DIM].astype(BF16)
        o_ref[1, :, cs] = ab[:, C_GROUP_DIM:].astype(BF16)


def _fourier_chan(x, g_pre, chan_cs, batch, seq):
    tm = min(ROW_TM, seq)
    tiles = seq // tm
    vmem = (2 * tm * D * 4 + tm * D * 2 + 2 * 2 * tm * D * 2) // MIB + 8
    return pl.pallas_call(
        _fourier_chan_kernel,
        grid=(batch, tiles),
        in_specs=[
            pl.BlockSpec((tm, D), lambda b, i: (b * tiles + i, 0)),
            pl.BlockSpec((1, D), lambda b, i: (0, 0)),
            pl.BlockSpec((C_GROUP_DIM, 2 * C_GROUP_DIM), lambda b, i: (0, 0)),
        ],
        out_specs=pl.BlockSpec((None, 2, tm, D), lambda b, i: (b, 0, i, 0)),
        out_shape=jax.ShapeDtypeStruct((batch, 2, seq, D), BF16),
        scratch_shapes=[pltpu.VMEM((tm, D), BF16)],
        compiler_params=_params(("parallel", "parallel"), vmem),
        name="fourier_chan",
    )(x, g_pre, chan_cs)


def _pos_factor_tables(seq):
    k = jnp.arange(seq, dtype=jnp.int32)

    def cos_sin(j):
        ang = ((j[:, None] * k[None, :]) % seq).astype(F32) * (2.0 * math.pi / seq)
        return jnp.cos(ang), jnp.sin(ang)

    ca, sa = cos_sin(jnp.arange(seq // LANE, dtype=jnp.int32) * LANE)
    cb, sb = cos_sin(jnp.arange(LANE, dtype=jnp.int32))
    return (jnp.concatenate([ca, -sa], axis=1), jnp.concatenate([-sa, -ca], axis=1),
            jnp.concatenate([cb, cb], axis=1), jnp.concatenate([sb, sb], axis=1))


def _fourier_pos_kernel(p_ref, q_ref, u_ref, v_ref, ab_ref, lo_ref, hi_ref, t_ref, acc_ref, *, norm, groups):
    i = pl.program_id(1)
    k = pl.program_id(2)
    half = pl.num_programs(2) // 2

    @pl.when(k == 0)
    def _():
        acc_ref[...] = jnp.zeros_like(acc_ref)

    first = (i % (p_ref.shape[0] // groups)) * groups
    for g in range(groups):
        t_ref[g * LANE:(g + 1) * LANE, :] = (p_ref[pl.ds(first + g, 1), :] * u_ref[...]
                                             + q_ref[pl.ds(first + g, 1), :] * v_ref[...]).astype(BF16)
    acc_ref[k // half] += jnp.dot(t_ref[...], ab_ref[...], preferred_element_type=F32)

    @pl.when(k == pl.num_programs(2) - 1)
    def _():
        def fn(rs):
            c, s = acc_ref[0, rs, :], acc_ref[1, rs, :]
            lo_ref[rs, :] = ((c + s) * norm).astype(BF16)
            hi_ref[rs, :] = ((c - s) * norm).astype(BF16)
        _row_chunks(t_ref.shape[0], fn)


def _fourier_pos(tables, ab, batch, seq):
    half = seq // 2
    tm = min(DFT_TM, half)
    tk = min(DFT_TK, seq)
    tiles = half // tm + 1
    ab2 = ab.reshape(batch * 2 * seq, D)
    k_tiles = 2 * seq // tk
    groups = tm // LANE
    window = max(groups, 8)
    vmem = (tm * tk * 2 + 4 * LANE * tk * 4 + 2 * tk * D * 2 + 2 * tm * D * 4 + 4 * tm * D * 2) // MIB + 8
    coarse = pl.BlockSpec((window, tk), lambda b, i, k: (i * groups // window, k))
    fine = pl.BlockSpec((LANE, tk), lambda b, i, k: (0, k))
    out = pl.BlockSpec((tm, D), lambda b, i, k: (b * tiles + i, 0))
    rows = jax.ShapeDtypeStruct((batch * tiles * tm, D), BF16)
    lo, hi = pl.pallas_call(
        functools.partial(_fourier_pos_kernel, norm=(seq * C_GROUP_DIM) ** -0.5, groups=groups),
        grid=(batch, tiles, k_tiles),
        in_specs=[coarse, coarse, fine, fine,
                  pl.BlockSpec((tk, D), lambda b, i, k: (b * k_tiles + k, 0))],
        out_specs=[out, out],
        out_shape=[rows, rows],
        scratch_shapes=[pltpu.VMEM((tm, tk), BF16), pltpu.VMEM((2, tm, D), F32)],
        compiler_params=_params(("parallel", "parallel", "arbitrary"), vmem),
        name="fourier_pos",
    )(*tables, ab2)
    lo = lo.reshape(batch, tiles * tm, D)
    hi = hi.reshape(batch, tiles * tm, D)
    mixed = jnp.concatenate([lo[:, :half + 1], jnp.flip(hi[:, 1:half], axis=1)], axis=1)
    return mixed.reshape(batch * seq, D)


def _xattn_kernel(x_ref, gpre_ref, wq_ref, kv_ref, wo_ref, gpost_ref, o_ref, h_ref, q_ref, a_ref, y_ref):
    _norm_rows_to(h_ref, x_ref, gpre_ref)
    q_ref[...] = (jnp.dot(h_ref[...], wq_ref[...], preferred_element_type=F32)
                  * (X_HEAD_DIM ** -0.5)).astype(BF16)

    for h in range(X_HEADS):
        cs = slice(h * X_HEAD_DIM, (h + 1) * X_HEAD_DIM)
        vs = slice(X_WIDTH + h * X_HEAD_DIM, X_WIDTH + (h + 1) * X_HEAD_DIM)
        s = lax.dot_general(q_ref[:, cs], kv_ref[:, cs], (((1,), (1,)), ((), ())),
                            preferred_element_type=F32)
        e = jnp.exp(s - jnp.max(s, axis=-1, keepdims=True))
        pv = jnp.dot(e.astype(BF16), kv_ref[:, vs], preferred_element_type=F32)
        a_ref[:, cs] = (pv / jnp.sum(e, axis=-1, keepdims=True)).astype(BF16)
    y_ref[...] = jnp.dot(a_ref[...], wo_ref[...], preferred_element_type=F32)
    _postnorm_residual_to(o_ref, x_ref, y_ref, gpost_ref, 1.0)


def _xattn(x, g_pre, w_q, kv, w_out, g_post, seq):
    n = x.shape[0]
    tm = min(ROW_TM, seq)
    tiles = seq // tm
    vmem = (2 * 2 * tm * D * 4 + tm * D * (2 + 4) + 4 * D * X_WIDTH * 2 + 2 * tm * X_WIDTH * 2) // MIB + 8
    return pl.pallas_call(
        _xattn_kernel,
        grid=(n // tm,),
        in_specs=[
            pl.BlockSpec((tm, D), lambda i: (i, 0)),
            pl.BlockSpec((1, D), lambda i: (0, 0)),
            pl.BlockSpec((D, X_WIDTH), lambda i: (0, 0)),
            pl.BlockSpec((N_MEM, 2 * X_WIDTH), lambda i: (i // tiles, 0)),
            pl.BlockSpec((X_WIDTH, D), lambda i: (0, 0)),
            pl.BlockSpec((1, D), lambda i: (0, 0)),
        ],
        out_specs=pl.BlockSpec((tm, D), lambda i: (i, 0)),
        out_shape=jax.ShapeDtypeStruct((n, D), F32),
        scratch_shapes=[pltpu.VMEM((tm, D), BF16), pltpu.VMEM((tm, X_WIDTH), BF16),
                        pltpu.VMEM((tm, X_WIDTH), BF16), pltpu.VMEM((tm, D), F32)],
        compiler_params=_params(("parallel",), vmem),
        name="xattn",
    )(x, g_pre, w_q, kv, w_out, g_post)


def _prepare(p):
    pad = FF_PAD - D_FF
    w = dict(p)
    w["ffn_w_gate"] = jnp.pad(p["ffn_w_gate"].astype(BF16), ((0, 0), (0, 0), (0, 0), (0, pad)))
    w["ffn_w_up"] = jnp.pad(p["ffn_w_up"].astype(BF16), ((0, 0), (0, 0), (0, 0), (0, pad)))
    w["ffn_w_down"] = jnp.pad(p["ffn_w_down"].astype(BF16), ((0, 0), (0, 0), (0, pad), (0, 0)))
    for name in ("a_w_in", "a_w_s", "a_w_out", "b_w_qkv", "b_w_out", "c_w_out", "x_w_q", "x_w_kv", "x_w_out"):
        w[name] = p[name].astype(BF16)
    w["a_b_s_t"] = jnp.swapaxes(p["a_b_s"], 1, 2)
    cc, sc = _dft_tables(C_GROUP_DIM)
    w["chan_cs"] = jnp.concatenate([cc, sc], axis=1).astype(BF16)
    return w


def _row(v):
    return v.reshape(1, -1)


def _trunk(x3, mem3, w, depth):
    batch, seq, _ = x3.shape
    x = x3.reshape(batch * seq, D)
    mem = mem3.reshape(batch * N_MEM, D)
    n_mixers = 3
    rot = pos_table = None
    for i in range(depth):
        kind, slot = i % n_mixers, i // n_mixers
        pre, post = w["ln_pre"][i], w["ln_post"][i]
        x = _ffn(x, _row(pre[0]), w["ffn_w_gate"][i, 0], w["ffn_w_up"][i, 0], w["ffn_w_down"][i, 0], _row(post[0]))
        if kind == 0:
            uv = _gmlp_in(x, _row(pre[1]), w["a_w_in"][slot], _row(w["a_ln_g"][slot]), _row(w["a_ln_b"][slot]))
            x = _gmlp_out(x, uv, w["a_w_s"][slot], w["a_b_s_t"][slot], w["a_w_out"][slot], _row(post[1]))
        elif kind == 1:
            if rot is None:
                rot = _rotary_tables(seq)
            qv, kt = _qkv(x, _row(pre[1]), w["b_w_qkv"][slot], rot, batch, seq)
            o = _diffattn(qv, kt, _row(w["b_lam_q1"][slot]), _row(w["b_lam_k1"][slot]), _row(w["b_lam_q2"][slot]),
                          _row(w["b_lam_k2"][slot]), _row(w["b_subln"][slot]), batch, seq, i)
            x = _proj_res(x, o, w["b_w_out"][slot], _row(post[1]))
        else:
            if pos_table is None:
                pos_table = _pos_factor_tables(seq)
            ab = _fourier_chan(x, _row(pre[1]), w["chan_cs"], batch, seq)
            mixed = _fourier_pos(pos_table, ab, batch, seq)
            x = _proj_res(x, mixed, w["c_w_out"][slot], _row(post[1]))
        kv = _norm_mm(mem, _row(w["ln_mem"][i]), w["x_w_kv"][i])
        x = _xattn(x, _row(pre[2]), w["x_w_q"][i], kv, w["x_w_out"][i], _row(post[2]), seq)
        x = _ffn(x, _row(pre[3]), w["ffn_w_gate"][i, 1], w["ffn_w_up"][i, 1], w["ffn_w_down"][i, 1], _row(post[3]))
    return x.reshape(batch, seq, D)


def kernel(x_prompt, x_sample, mem_prompt, mem_sample, ln_pre, ln_post, ln_mem, ffn_w_gate, ffn_w_up, ffn_w_down, a_w_in, a_ln_g, a_ln_b, a_w_s, a_b_s, a_w_out, b_w_qkv, b_lam_q1, b_lam_k1, b_lam_q2, b_lam_k2, b_subln, b_w_out, c_w_out, x_w_q, x_w_kv, x_w_out):
    params = {
        "ln_pre": ln_pre, "ln_post": ln_post, "ln_mem": ln_mem,
        "ffn_w_gate": ffn_w_gate, "ffn_w_up": ffn_w_up, "ffn_w_down": ffn_w_down,
        "a_w_in": a_w_in, "a_ln_g": a_ln_g, "a_ln_b": a_ln_b, "a_w_s": a_w_s, "a_b_s": a_b_s, "a_w_out": a_w_out,
        "b_w_qkv": b_w_qkv, "b_lam_q1": b_lam_q1, "b_lam_k1": b_lam_k1, "b_lam_q2": b_lam_q2,
        "b_lam_k2": b_lam_k2, "b_subln": b_subln, "b_w_out": b_w_out,
        "c_w_out": c_w_out, "x_w_q": x_w_q, "x_w_kv": x_w_kv, "x_w_out": x_w_out,
    }
    depth = ln_pre.shape[0]
    w = _prepare(params)
    return (_trunk(x_prompt, mem_prompt, w, depth), _trunk(x_sample, mem_sample, w, depth))
```

```python
import functools
import math

import jax
import jax.numpy as jnp
from jax import lax
from jax.experimental import pallas as pl
from jax.experimental.pallas import tpu as pltpu

F32 = jnp.float32
BF16 = jnp.bfloat16

D = 2048
D_FF = 5504
LANE = 128
FF_PAD = 5632
CHUNK = 128
A_GROUPS = 8
A_GROUP_DIM = D // A_GROUPS
B_HEADS = 8
B_HEAD_DIM = 128
B_VALUE_DIM = 256
ROT_DIM = 32
ROPE_THETA = 500000.0
C_GROUPS = 8
C_GROUP_DIM = D // C_GROUPS
X_HEADS = 4
X_HEAD_DIM = 128
X_WIDTH = X_HEADS * X_HEAD_DIM
N_MEM = 256
NORM_EPS = 1e-6
SUBLN_EPS = 1e-5
LN_EPS = 1e-5
MIB = 1024 * 1024
VMEM_CAP_MIB = 60

FFN_TM, FFN_TF = 1024, 512
FFN_PIECES = 8
ROW_TM = 512
ROW_CHUNK = 128
NORM_CHUNK = ROW_TM
COL_BLOCK = 512
ATT_TQ = 512
ATT_SCORE_MIB = 16
ATT_TILE = 128 * 1024
ATT_UNROLL = 8
Q_SCALE = B_HEAD_DIM ** -0.5 * math.log2(math.e)
DFT_TM, DFT_TK = 512, 2048


def _params(semantics, vmem_mib, flags=None):
    return pltpu.CompilerParams(dimension_semantics=semantics, flags=flags,
                                vmem_limit_bytes=min(vmem_mib, VMEM_CAP_MIB) * MIB)


def _row_chunks(n_rows, fn, rc=None):
    rc = min(rc or ROW_CHUNK, n_rows)

    def body(c, carry):
        fn(pl.ds(pl.multiple_of(c * rc, rc), rc))
        return carry

    lax.fori_loop(0, n_rows // rc, body, 0)


def _rms(x, g, eps):
    return x * lax.rsqrt(jnp.mean(x * x, axis=-1, keepdims=True) + eps) * g


def _norm_rows_to(h_ref, x_ref, g_ref):
    def fn(rs):
        h_ref[rs, :] = _rms(x_ref[rs, :], g_ref[...], NORM_EPS).astype(BF16)
    _row_chunks(x_ref.shape[0], fn, NORM_CHUNK)


def _postnorm_residual_to(o_ref, x_ref, y_ref, g_ref, scale):
    def fn(rs):
        o_ref[rs, :] = x_ref[rs, :] + scale * _rms(y_ref[rs, :], g_ref[...], NORM_EPS)
    _row_chunks(x_ref.shape[0], fn, NORM_CHUNK)


def _ffn_kernel(xn_ref, xp_ref, gpre_ref, wg_ref, wu_ref, wd_ref, gpost_ref, o_ref, h_ref, acc_ref,
                *, n_tiles, pieces):
    i = pl.program_id(0)
    f = pl.program_id(1)
    side = i % 2
    main = 1 - side
    pr = xn_ref.shape[0]
    rows = pl.ds(pl.multiple_of(jnp.minimum(f, pieces - 1) * pr, pr), pr)

    def norm_in():
        h_ref[side, rows, :] = _rms(xn_ref[...], gpre_ref[...], NORM_EPS).astype(BF16)

    def finish():
        o_ref[...] = xp_ref[...] + 0.5 * _rms(acc_ref[side, rows, :], gpost_ref[...], NORM_EPS)

    def step(first, with_norms):
        h = h_ref[main]
        gate = jnp.dot(h, wg_ref[...], preferred_element_type=F32)
        up = jnp.dot(h, wu_ref[...], preferred_element_type=F32)
        a = (gate * jax.nn.sigmoid(gate) * up).astype(BF16)
        d = jnp.dot(a, wd_ref[...], preferred_element_type=F32)
        if first:
            acc_ref[main] = d
        else:
            acc_ref[main] += d
        if with_norms:
            norm_in()
            finish()

    has_piece = f < pieces

    @pl.when(jnp.logical_and(i == 0, has_piece))
    def _():
        norm_in()

        @pl.when(f == 0)
        def _():
            acc_ref[...] = jnp.zeros_like(acc_ref)

    in_main = jnp.logical_and(i >= 1, i <= n_tiles)

    @pl.when(jnp.logical_and(in_main, f == 0))
    def _():
        step(True, True)

    @pl.when(jnp.logical_and(in_main, jnp.logical_and(f > 0, has_piece)))
    def _():
        step(False, True)

    @pl.when(jnp.logical_and(in_main, f >= pieces))
    def _():
        step(False, False)

    @pl.when(jnp.logical_and(i == n_tiles + 1, has_piece))
    def _():
        finish()


def _ffn(x, g_pre, wg, wu, wd, g_post):
    n = x.shape[0]
    tm = min(FFN_TM, n)
    tf = FFN_TF
    fp = wg.shape[1]
    n_tiles, nf = n // tm, fp // tf
    pieces = FFN_PIECES
    pr = tm // pieces
    assert pieces <= nf and pr % 16 == 0

    def piece(tile, f):
        return (jnp.where(tile < 0, 0, jnp.minimum(tile, n_tiles - 1) * pieces + jnp.minimum(f, pieces - 1)), 0)

    def wcol(i, f):
        return jnp.where(jnp.logical_and(i >= 1, i <= n_tiles), f, 0)

    vmem = (3 * 2 * pr * D * 4 + 2 * tm * D * (2 + 4) + 2 * 3 * D * tf * 2 + 4 * tm * tf * 4) // MIB + 6
    return pl.pallas_call(
        functools.partial(_ffn_kernel, n_tiles=n_tiles, pieces=pieces),
        grid=(n_tiles + 2, nf),
        in_specs=[
            pl.BlockSpec((pr, D), lambda i, f: piece(i, f)),
            pl.BlockSpec((pr, D), lambda i, f: piece(i - 2, f)),
            pl.BlockSpec((1, D), lambda i, f: (0, 0)),
            pl.BlockSpec((D, tf), lambda i, f: (0, wcol(i, f))),
            pl.BlockSpec((D, tf), lambda i, f: (0, wcol(i, f))),
            pl.BlockSpec((tf, D), lambda i, f: (wcol(i, f), 0)),
            pl.BlockSpec((1, D), lambda i, f: (0, 0)),
        ],
        out_specs=pl.BlockSpec((pr, D), lambda i, f: piece(i - 2, f)),
        out_shape=jax.ShapeDtypeStruct((n, D), F32),
        scratch_shapes=[pltpu.VMEM((2, tm, D), BF16), pltpu.VMEM((2, tm, D), F32)],
        compiler_params=_params(("arbitrary", "arbitrary"), vmem),
        name="ffn",
    )(x, x, g_pre, wg, wu, wd, g_post)


def _norm_mm_kernel(x_ref, g_ref, w_ref, o_ref, h_ref):
    _norm_rows_to(h_ref, x_ref, g_ref)
    o_ref[...] = jnp.dot(h_ref[...], w_ref[...], preferred_element_type=F32).astype(o_ref.dtype)


def _norm_mm(x, g, w):
    n, n_out = x.shape[0], w.shape[1]
    tm = min(ROW_TM, n)
    tn = min(1024, n_out)
    vmem = (2 * tm * D * 4 + tm * D * 2 + 2 * D * tn * 2 + 2 * tm * tn * 2 + tm * tn * 4) // MIB + 6
    return pl.pallas_call(
        _norm_mm_kernel,
        grid=(n_out // tn, n // tm),
        in_specs=[
            pl.BlockSpec((tm, D), lambda j, i: (i, 0)),
            pl.BlockSpec((1, D), lambda j, i: (0, 0)),
            pl.BlockSpec((D, tn), lambda j, i: (0, j)),
        ],
        out_specs=pl.BlockSpec((tm, tn), lambda j, i: (i, j)),
        out_shape=jax.ShapeDtypeStruct((n, n_out), BF16),
        scratch_shapes=[pltpu.VMEM((tm, D), BF16)],
        compiler_params=_params(("parallel", "parallel"), vmem),
        name="norm_mm",
    )(x, g, w)


def _proj_res_kernel(x_ref, a_ref, w_ref, g_ref, o_ref, y_ref):
    y_ref[...] = jnp.dot(a_ref[...], w_ref[...], preferred_element_type=F32)
    _postnorm_residual_to(o_ref, x_ref, y_ref, g_ref, 1.0)


def _proj_res(x, a, w, g_post):
    n, k = a.shape
    tm = min(ROW_TM, n)
    vmem = (2 * 2 * tm * D * 4 + 2 * tm * k * 2 + 2 * k * D * 2 + tm * D * 4) // MIB + 6
    return pl.pallas_call(
        _proj_res_kernel,
        grid=(n // tm,),
        in_specs=[
            pl.BlockSpec((tm, D), lambda i: (i, 0)),
            pl.BlockSpec((tm, k), lambda i: (i, 0)),
            pl.BlockSpec((k, D), lambda i: (0, 0)),
            pl.BlockSpec((1, D), lambda i: (0, 0)),
        ],
        out_specs=pl.BlockSpec((tm, D), lambda i: (i, 0)),
        out_shape=jax.ShapeDtypeStruct((n, D), F32),
        scratch_shapes=[pltpu.VMEM((tm, D), F32)],
        compiler_params=_params(("parallel",), vmem),
        name="proj_res",
    )(x, a, w, g_post)


def _gelu(y):
    return 0.5 * y * (1.0 + lax.erf(y * (0.5 ** 0.5)))


def _gmlp_in_kernel(x_ref, g_ref, w_ref, lg_ref, lb_ref, o_ref, h_ref, y_ref):
    j = pl.program_id(0)
    _norm_rows_to(h_ref, x_ref, g_ref)

    def gelu_block(c):
        cs = slice(c * COL_BLOCK, (c + 1) * COL_BLOCK)
        return cs, _gelu(jnp.dot(h_ref[...], w_ref[:, cs], preferred_element_type=F32))

    @pl.when(j == 0)
    def _():
        for c in range(D // COL_BLOCK):
            cs, y = gelu_block(c)
            o_ref[:, cs] = y.astype(BF16)

    @pl.when(j == 1)
    def _():
        for c in range(D // COL_BLOCK):
            cs, y = gelu_block(c)
            y_ref[:, cs] = y

        def fn(rs):
            v = y_ref[rs, :]
            vc = v - jnp.mean(v, axis=-1, keepdims=True)
            var = jnp.mean(vc * vc, axis=-1, keepdims=True)
            o_ref[rs, :] = (vc * lax.rsqrt(var + LN_EPS) * lg_ref[...] + lb_ref[...]).astype(BF16)
        _row_chunks(x_ref.shape[0], fn, NORM_CHUNK)


def _gmlp_in(x, g_pre, w_in, ln_g, ln_b):
    n = x.shape[0]
    tm = min(ROW_TM, n)
    vmem = (2 * tm * D * 4 + tm * D * 2 + 2 * D * D * 2 + 2 * tm * D * 2 + tm * D * 4) // MIB + 6
    return pl.pallas_call(
        _gmlp_in_kernel,
        grid=(2, n // tm),
        in_specs=[
            pl.BlockSpec((tm, D), lambda j, i: (i, 0)),
            pl.BlockSpec((1, D), lambda j, i: (0, 0)),
            pl.BlockSpec((D, D), lambda j, i: (0, j)),
            pl.BlockSpec((1, D), lambda j, i: (0, 0)),
            pl.BlockSpec((1, D), lambda j, i: (0, 0)),
        ],
        out_specs=pl.BlockSpec((None, tm, D), lambda j, i: (j, i, 0)),
        out_shape=jax.ShapeDtypeStruct((2, n, D), BF16),
        scratch_shapes=[pltpu.VMEM((tm, D), BF16), pltpu.VMEM((tm, D), F32)],
        compiler_params=_params(("parallel", "parallel"), vmem),
        name="gmlp_in",
    )(x, g_pre, w_in, ln_g, ln_b)


def _gmlp_out_kernel(x_ref, u_ref, v_ref, ws_ref, bs_ref, w_ref, g_ref, o_ref, a_ref, y_ref):
    for c in range(x_ref.shape[0] // CHUNK):
        rs = slice(c * CHUNK, (c + 1) * CHUNK)
        for g in range(A_GROUPS):
            cs = slice(g * A_GROUP_DIM, (g + 1) * A_GROUP_DIM)
            mixed = jnp.dot(ws_ref[g], v_ref[rs, cs], preferred_element_type=F32) + bs_ref[:, g:g + 1]
            a_ref[rs, cs] = (u_ref[rs, cs].astype(F32) * mixed).astype(BF16)
    y_ref[...] = jnp.dot(a_ref[...], w_ref[...], preferred_element_type=F32)
    _postnorm_residual_to(o_ref, x_ref, y_ref, g_ref, 1.0)


def _gmlp_out(x, uv, w_s, b_s_t, w_out, g_post):
    n = x.shape[0]
    tm = min(ROW_TM, n)
    assert ROW_CHUNK == CHUNK and tm % CHUNK == 0
    vmem = (2 * 2 * tm * D * 4 + 2 * 2 * tm * D * 2 + 2 * D * D * 2 + tm * D * (2 + 4)) // MIB + 6
    return pl.pallas_call(
        _gmlp_out_kernel,
        grid=(n // tm,),
        in_specs=[
            pl.BlockSpec((tm, D), lambda i: (i, 0)),
            pl.BlockSpec((None, tm, D), lambda i: (0, i, 0)),
            pl.BlockSpec((None, tm, D), lambda i: (1, i, 0)),
            pl.BlockSpec((A_GROUPS, CHUNK, CHUNK), lambda i: (0, 0, 0)),
            pl.BlockSpec((CHUNK, A_GROUPS), lambda i: (0, 0)),
            pl.BlockSpec((D, D), lambda i: (0, 0)),
            pl.BlockSpec((1, D), lambda i: (0, 0)),
        ],
        out_specs=pl.BlockSpec((tm, D), lambda i: (i, 0)),
        out_shape=jax.ShapeDtypeStruct((n, D), F32),
        scratch_shapes=[pltpu.VMEM((tm, D), BF16), pltpu.VMEM((tm, D), F32)],
        compiler_params=_params(("parallel",), vmem),
        name="gmlp_out",
    )(x, uv, uv, w_s, b_s_t, w_out, g_post)


def _rotary_tables(s):
    half = ROT_DIM // 2
    inv_freq = 1.0 / (ROPE_THETA ** (jnp.arange(0, ROT_DIM, 2, dtype=F32) / ROT_DIM))
    ang = jnp.arange(s, dtype=F32)[:, None] * inv_freq[None, :]
    cos, sin = jnp.cos(ang), jnp.sin(ang)
    rest = B_HEAD_DIM - ROT_DIM
    ct = jnp.concatenate([cos, cos, jnp.ones((s, rest), F32)], axis=1)
    sa = jnp.concatenate([-sin, jnp.zeros((s, half + rest), F32)], axis=1)
    sb = jnp.concatenate([jnp.zeros((s, half), F32), sin, jnp.zeros((s, rest), F32)], axis=1)
    return ct, sa, sb


def _qkv_kernel(x_ref, g_ref, w_ref, ct_ref, sa_ref, sb_ref, qv_ref, kt_ref, h_ref, t_ref):
    j = pl.program_id(0)
    _norm_rows_to(h_ref, x_ref, g_ref)

    def block(c):
        return jnp.dot(h_ref[...], w_ref[:, c * COL_BLOCK:(c + 1) * COL_BLOCK], preferred_element_type=F32)

    def rotary_blocks(scale, write):
        t_ref[0] = ct_ref[...] * scale
        t_ref[1] = sa_ref[...] * scale
        t_ref[2] = sb_ref[...] * scale
        for c in range(D // COL_BLOCK):
            y = block(c)
            for k in range(COL_BLOCK // LANE):
                yc = y[:, k * LANE:(k + 1) * LANE]
                rot = (yc * t_ref[0] + pltpu.roll(yc, LANE - ROT_DIM // 2, 1) * t_ref[1]
                       + pltpu.roll(yc, ROT_DIM // 2, 1) * t_ref[2])
                write(c * COL_BLOCK + k * LANE, rot)

    @pl.when(j == 0)
    def _():
        def write(lo, rot):
            qv_ref[:, lo:lo + LANE] = rot.astype(BF16)
        rotary_blocks(Q_SCALE, write)

    @pl.when(j == 1)
    def _():
        def write(lo, rot):
            rt = rot.T.astype(BF16)
            tk = kt_ref.shape[2]
            for s in range(kt_ref.shape[0]):
                kt_ref[s, lo:lo + LANE, :] = rt[:, s * tk:(s + 1) * tk]
        rotary_blocks(1.0, write)

    @pl.when(j == 2)
    def _():
        for c in range(D // COL_BLOCK):
            qv_ref[:, c * COL_BLOCK:(c + 1) * COL_BLOCK] = block(c).astype(BF16)


def _att_tiles(seq):
    tq = min(ATT_TQ, seq, ATT_SCORE_MIB * MIB // (8 * seq))
    return tq, min(ATT_TILE // tq, seq, ROW_TM)


def _qkv(x, g_pre, w_qkv, tables, batch, seq):
    n = x.shape[0]
    tm = min(ROW_TM, seq)
    tk = _att_tiles(seq)[1]
    sub = tm // tk
    pos_tiles = seq // tm
    tiles = n // tm
    ct, sa, sb = tables
    vmem = (2 * tm * D * 4 + tm * D * 2 + 2 * D * D * 2 + 4 * tm * D * 2 + tm * COL_BLOCK * 4) // MIB + 8
    tab = pl.BlockSpec((tm, LANE), lambda j, i: (i % pos_tiles, 0))

    def qv_block(j, i):
        return (j // 2, jnp.where(j == 1, tiles - 1, i), 0)

    def kt_block(j, i):
        t = jnp.where(j == 0, 0, jnp.where(j == 1, i, tiles - 1))
        return (t // pos_tiles, t % pos_tiles, 0, 0)

    return pl.pallas_call(
        _qkv_kernel,
        grid=(3, tiles),
        in_specs=[
            pl.BlockSpec((tm, D), lambda j, i: (i, 0)),
            pl.BlockSpec((1, D), lambda j, i: (0, 0)),
            pl.BlockSpec((D, D), lambda j, i: (0, j)),
            tab, tab, tab,
        ],
        out_specs=[pl.BlockSpec((None, tm, D), qv_block),
                   pl.BlockSpec((None, sub, D, tk), kt_block)],
        out_shape=[jax.ShapeDtypeStruct((2, n, D), BF16),
                   jax.ShapeDtypeStruct((batch, seq // tk, D, tk), BF16)],
        scratch_shapes=[pltpu.VMEM((tm, D), BF16), pltpu.VMEM((3, tm, LANE), F32)],
        compiler_params=_params(("arbitrary", "arbitrary"), vmem),
        name="qkv",
    )(x, g_pre, w_qkv, ct, sa, sb)


def _diffattn_kernel(q_ref, kt_ref, v_ref, lq1_ref, lk1_ref, lq2_ref, lk2_ref, sg_ref, o_ref,
                     s_ref, m_ref, l_ref, p_ref, acc_ref, *, tk, lam_init):
    seq = v_ref.shape[0]
    lane_blocks = tk // LANE
    unroll = max(1, min(ATT_UNROLL, seq // tk // 2))
    lam = (jnp.exp(jnp.sum(lq1_ref[...] * lk1_ref[...], axis=-1, keepdims=True))
           - jnp.exp(jnp.sum(lq2_ref[...] * lk2_ref[...], axis=-1, keepdims=True)) + lam_init)
    m_ref[...] = jnp.full_like(m_ref, -jnp.inf)
    l_ref[...] = jnp.zeros_like(l_ref)
    acc_ref[...] = jnp.zeros_like(acc_ref)

    def scores(t, carry):
        for c in range(2):
            cs = slice(c * B_HEAD_DIM, (c + 1) * B_HEAD_DIM)
            s = jnp.dot(q_ref[:, cs], kt_ref[t, cs, :], preferred_element_type=F32)
            s_ref[c, t] = s
            m = m_ref[c]
            for j in range(lane_blocks):
                m = jnp.maximum(m, s[:, j * LANE:(j + 1) * LANE])
            m_ref[c] = m
        return carry

    lax.fori_loop(0, seq // tk, scores, 0, unroll=unroll)
    for c in range(2):
        m_ref[c] = jnp.broadcast_to(jnp.max(m_ref[c], axis=-1, keepdims=True), m_ref.shape[1:])

    def weights(t, carry):
        ks = pl.ds(pl.multiple_of(t * tk, tk), tk)
        for c in range(2):
            m = m_ref[c]
            l = l_ref[c]
            for j in range(lane_blocks):
                ls = slice(j * LANE, (j + 1) * LANE)
                p = jnp.exp2(s_ref[c, t, :, ls] - m)
                l = l + p
                p_ref[c, :, ls] = p.astype(BF16)
            l_ref[c] = l
            acc_ref[c] += jnp.dot(p_ref[c], v_ref[ks, :], preferred_element_type=F32)
        return carry

    lax.fori_loop(0, seq // tk, weights, 0, unroll=unroll)
    l0 = jnp.sum(l_ref[0], axis=-1, keepdims=True)
    l1 = jnp.sum(l_ref[1], axis=-1, keepdims=True)
    o = acc_ref[0] / l0 - lam * (acc_ref[1] / l1)
    o_ref[...] = (_rms(o, sg_ref[...], SUBLN_EPS) * (1.0 - lam_init)).astype(BF16)


def _diffattn(qv, kt, lam_q1, lam_k1, lam_q2, lam_k2, subln_g, batch, seq, layer_idx):
    n = batch * seq
    tq, tk = _att_tiles(seq)
    q_tiles = seq // tq
    hw = 2 * B_HEAD_DIM
    lam_init = 0.8 - 0.6 * math.exp(-0.3 * layer_idx)
    vec = pl.BlockSpec((1, B_HEAD_DIM), lambda b, h, i: (0, 0))
    vmem = (2 * 2 * seq * hw * 2 + 4 * tq * hw * 2 + 2 * tq * seq * 4 + 2 * tq * hw * 4 + 4 * tq * tk * 4) // MIB + 8
    return pl.pallas_call(
        functools.partial(_diffattn_kernel, tk=tk, lam_init=lam_init),
        grid=(batch, B_HEADS, q_tiles),
        in_specs=[
            pl.BlockSpec((None, tq, hw), lambda b, h, i: (0, b * q_tiles + i, h)),
            pl.BlockSpec((None, seq // tk, hw, tk), lambda b, h, i: (b, 0, h, 0)),
            pl.BlockSpec((None, seq, hw), lambda b, h, i: (1, b, h)),
            vec, vec, vec, vec,
            pl.BlockSpec((1, B_VALUE_DIM), lambda b, h, i: (0, 0)),
        ],
        out_specs=pl.BlockSpec((tq, hw), lambda b, h, i: (b * q_tiles + i, h)),
        out_shape=jax.ShapeDtypeStruct((n, D), BF16),
        scratch_shapes=[pltpu.VMEM((2, seq // tk, tq, tk), F32),
                        pltpu.VMEM((2, tq, LANE), F32), pltpu.VMEM((2, tq, LANE), F32),
                        pltpu.VMEM((2, tq, tk), BF16), pltpu.VMEM((2, tq, B_VALUE_DIM), F32)],
        compiler_params=_params(("parallel", "parallel", "arbitrary"), vmem),
        name="diffattn",
    )(qv, kt, qv, lam_q1, lam_k1, lam_q2, lam_k2, subln_g)


def _dft_tables(n):
    idx = jnp.arange(n, dtype=jnp.int32)
    ang = ((idx[:, None] * idx[None, :]) % n).astype(F32) * (2.0 * math.pi / n)
    return jnp.cos(ang), jnp.sin(ang)


def _fourier_chan_kernel(x_ref, g_ref, cs_ref, o_ref, h_ref):
    _norm_rows_to(h_ref, x_ref, g_ref)
    for g in range(C_GROUPS):
        cs = slice(g * C_GROUP_DIM, (g + 1) * C_GROUP_DIM)
        ab = jnp.dot(h_ref[:, cs], cs_ref[...], preferred_element_type=F32)
        o_ref[0, :, cs] = ab[:, :C_GROUP_DIM].astype(BF16)
        o_ref[1, :, cs] = ab[:, C_GROUP_DIM:].astype(BF16)


def _fourier_chan(x, g_pre, chan_cs, batch, seq):
    tm = min(ROW_TM, seq)
    tiles = seq // tm
    vmem = (2 * tm * D * 4 + tm * D * 2 + 2 * 2 * tm * D * 2) // MIB + 8
    return pl.pallas_call(
        _fourier_chan_kernel,
        grid=(batch, tiles),
        in_specs=[
            pl.BlockSpec((tm, D), lambda b, i: (b * tiles + i, 0)),
            pl.BlockSpec((1, D), lambda b, i: (0, 0)),
            pl.BlockSpec((C_GROUP_DIM, 2 * C_GROUP_DIM), lambda b, i: (0, 0)),
        ],
        out_specs=pl.BlockSpec((None, 2, tm, D), lambda b, i: (b, 0, i, 0)),
        out_shape=jax.ShapeDtypeStruct((batch, 2, seq, D), BF16),
        scratch_shapes=[pltpu.VMEM((tm, D), BF16)],
        compiler_params=_params(("parallel", "parallel"), vmem),
        name="fourier_chan",
    )(x, g_pre, chan_cs)


def _pos_factor_tables(seq):
    k = jnp.arange(seq, dtype=jnp.int32)

    def cos_sin(j):
        ang = ((j[:, None] * k[None, :]) % seq).astype(F32) * (2.0 * math.pi / seq)
        return jnp.cos(ang), jnp.sin(ang)

    ca, sa = cos_sin(jnp.arange(seq // LANE, dtype=jnp.int32) * LANE)
    cb, sb = cos_sin(jnp.arange(LANE, dtype=jnp.int32))
    return (jnp.concatenate([ca, -sa], axis=1), jnp.concatenate([-sa, -ca], axis=1),
            jnp.concatenate([cb, cb], axis=1), jnp.concatenate([sb, sb], axis=1))


def _fourier_pos_kernel(p_ref, q_ref, u_ref, v_ref, ab_ref, lo_ref, hi_ref, t_ref, acc_ref, *, norm, groups):
    i = pl.program_id(1)
    k = pl.program_id(2)
    half = pl.num_programs(2) // 2

    @pl.when(k == 0)
    def _():
        acc_ref[...] = jnp.zeros_like(acc_ref)

    first = (i % (p_ref.shape[0] // groups)) * groups
    for g in range(groups):
        t_ref[g * LANE:(g + 1) * LANE, :] = (p_ref[pl.ds(first + g, 1), :] * u_ref[...]
                                             + q_ref[pl.ds(first + g, 1), :] * v_ref[...]).astype(BF16)
    acc_ref[k // half] += jnp.dot(t_ref[...], ab_ref[...], preferred_element_type=F32)

    @pl.when(k == pl.num_programs(2) - 1)
    def _():
        def fn(rs):
            c, s = acc_ref[0, rs, :], acc_ref[1, rs, :]
            lo_ref[rs, :] = ((c + s) * norm).astype(BF16)
            hi_ref[rs, :] = ((c - s) * norm).astype(BF16)
        _row_chunks(t_ref.shape[0], fn)


def _fourier_assemble_kernel(lo_ref, hia_ref, hib_ref, o_ref, stack_ref, *, lower_tiles):
    i = pl.program_id(1)
    tm = o_ref.shape[0]

    @pl.when(i < lower_tiles)
    def _():
        o_ref[...] = lo_ref[...]

    @pl.when(i >= lower_tiles)
    def _():
        stack_ref[0:tm, :] = hia_ref[...]
        stack_ref[tm:, :] = hib_ref[...]
        row = lax.broadcasted_iota(jnp.int32, (tm, 2 * tm), 0)
        col = lax.broadcasted_iota(jnp.int32, (tm, 2 * tm), 1)
        backwards = (col == tm - row).astype(BF16)
        o_ref[...] = jnp.dot(backwards, stack_ref[...], preferred_element_type=F32).astype(BF16)


def _fourier_assemble(lo, hi, batch, seq, tm):
    half_tiles = seq // 2 // tm
    src_tiles = half_tiles + 1
    all_tiles = seq // tm

    def lo_block(b, i):
        return (b * src_tiles + jnp.minimum(i, half_tiles - 1), 0)

    def hi_block(offset):
        return lambda b, i: (b * src_tiles + jnp.clip(all_tiles - i - 1, 0, half_tiles - 1) + offset, 0)

    return pl.pallas_call(
        functools.partial(_fourier_assemble_kernel, lower_tiles=half_tiles),
        grid=(batch, all_tiles),
        in_specs=[pl.BlockSpec((tm, D), lo_block),
                  pl.BlockSpec((tm, D), hi_block(0)),
                  pl.BlockSpec((tm, D), hi_block(1))],
        out_specs=pl.BlockSpec((tm, D), lambda b, i: (b * all_tiles + i, 0)),
        out_shape=jax.ShapeDtypeStruct((batch * seq, D), BF16),
        scratch_shapes=[pltpu.VMEM((2 * tm, D), BF16)],
        compiler_params=_params(("parallel", "parallel"), 8 * tm * D * 2 // MIB + 8),
        name="fourier_assemble",
    )(lo, hi, hi)


def _fourier_pos(tables, ab, batch, seq):
    half = seq // 2
    tm = min(DFT_TM, half)
    tk = min(DFT_TK, seq)
    tiles = half // tm + 1
    ab2 = ab.reshape(batch * 2 * seq, D)
    k_tiles = 2 * seq // tk
    groups = tm // LANE
    window = max(groups, 8)
    vmem = (tm * tk * 2 + 4 * LANE * tk * 4 + 2 * tk * D * 2 + 2 * tm * D * 4 + 4 * tm * D * 2) // MIB + 8
    coarse = pl.BlockSpec((window, tk), lambda b, i, k: (i * groups // window, k))
    fine = pl.BlockSpec((LANE, tk), lambda b, i, k: (0, k))
    out = pl.BlockSpec((tm, D), lambda b, i, k: (b * tiles + i, 0))
    rows = jax.ShapeDtypeStruct((batch * tiles * tm, D), BF16)
    lo, hi = pl.pallas_call(
        functools.partial(_fourier_pos_kernel, norm=(seq * C_GROUP_DIM) ** -0.5, groups=groups),
        grid=(batch, tiles, k_tiles),
        in_specs=[coarse, coarse, fine, fine,
                  pl.BlockSpec((tk, D), lambda b, i, k: (b * k_tiles + k, 0))],
        out_specs=[out, out],
        out_shape=[rows, rows],
        scratch_shapes=[pltpu.VMEM((tm, tk), BF16), pltpu.VMEM((2, tm, D), F32)],
        compiler_params=_params(("parallel", "parallel", "arbitrary"), vmem),
        name="fourier_pos",
    )(*tables, ab2)
    return _fourier_assemble(lo, hi, batch, seq, tm)


def _xattn_kernel(x_ref, gpre_ref, wq_ref, kv_ref, wo_ref, gpost_ref, o_ref, h_ref, q_ref, a_ref, y_ref):
    _norm_rows_to(h_ref, x_ref, gpre_ref)
    q_ref[...] = (jnp.dot(h_ref[...], wq_ref[...], preferred_element_type=F32)
                  * (X_HEAD_DIM ** -0.5)).astype(BF16)

    for h in range(X_HEADS):
        cs = slice(h * X_HEAD_DIM, (h + 1) * X_HEAD_DIM)
        vs = slice(X_WIDTH + h * X_HEAD_DIM, X_WIDTH + (h + 1) * X_HEAD_DIM)
        s = lax.dot_general(q_ref[:, cs], kv_ref[:, cs], (((1,), (1,)), ((), ())),
                            preferred_element_type=F32)
        e = jnp.exp(s - jnp.max(s, axis=-1, keepdims=True))
        pv = jnp.dot(e.astype(BF16), kv_ref[:, vs], preferred_element_type=F32)
        a_ref[:, cs] = (pv / jnp.sum(e, axis=-1, keepdims=True)).astype(BF16)
    y_ref[...] = jnp.dot(a_ref[...], wo_ref[...], preferred_element_type=F32)
    _postnorm_residual_to(o_ref, x_ref, y_ref, gpost_ref, 1.0)


def _xattn(x, g_pre, w_q, kv, w_out, g_post, seq):
    n = x.shape[0]
    tm = min(ROW_TM, seq)
    tiles = seq // tm
    vmem = (2 * 2 * tm * D * 4 + tm * D * (2 + 4) + 4 * D * X_WIDTH * 2 + 2 * tm * X_WIDTH * 2) // MIB + 8
    return pl.pallas_call(
        _xattn_kernel,
        grid=(n // tm,),
        in_specs=[
            pl.BlockSpec((tm, D), lambda i: (i, 0)),
            pl.BlockSpec((1, D), lambda i: (0, 0)),
            pl.BlockSpec((D, X_WIDTH), lambda i: (0, 0)),
            pl.BlockSpec((N_MEM, 2 * X_WIDTH), lambda i: (i // tiles, 0)),
            pl.BlockSpec((X_WIDTH, D), lambda i: (0, 0)),
            pl.BlockSpec((1, D), lambda i: (0, 0)),
        ],
        out_specs=pl.BlockSpec((tm, D), lambda i: (i, 0)),
        out_shape=jax.ShapeDtypeStruct((n, D), F32),
        scratch_shapes=[pltpu.VMEM((tm, D), BF16), pltpu.VMEM((tm, X_WIDTH), BF16),
                        pltpu.VMEM((tm, X_WIDTH), BF16), pltpu.VMEM((tm, D), F32)],
        compiler_params=_params(("parallel",), vmem),
        name="xattn",
    )(x, g_pre, w_q, kv, w_out, g_post)


def _prepare(p):
    pad = FF_PAD - D_FF
    w = dict(p)
    w["ffn_w_gate"] = jnp.pad(p["ffn_w_gate"].astype(BF16), ((0, 0), (0, 0), (0, 0), (0, pad)))
    w["ffn_w_up"] = jnp.pad(p["ffn_w_up"].astype(BF16), ((0, 0), (0, 0), (0, 0), (0, pad)))
    w["ffn_w_down"] = jnp.pad(p["ffn_w_down"].astype(BF16), ((0, 0), (0, 0), (0, pad), (0, 0)))
    for name in ("a_w_in", "a_w_s", "a_w_out", "b_w_qkv", "b_w_out", "c_w_out", "x_w_q", "x_w_kv", "x_w_out"):
        w[name] = p[name].astype(BF16)
    w["a_b_s_t"] = jnp.swapaxes(p["a_b_s"], 1, 2)
    cc, sc = _dft_tables(C_GROUP_DIM)
    w["chan_cs"] = jnp.concatenate([cc, sc], axis=1).astype(BF16)
    return w


def _row(v):
    return v.reshape(1, -1)


def _trunk(x3, mem3, w, depth):
    batch, seq, _ = x3.shape
    x = x3.reshape(batch * seq, D)
    mem = mem3.reshape(batch * N_MEM, D)
    n_mixers = 3
    rot = pos_table = None
    for i in range(depth):
        kind, slot = i % n_mixers, i // n_mixers
        pre, post = w["ln_pre"][i], w["ln_post"][i]
        x = _ffn(x, _row(pre[0]), w["ffn_w_gate"][i, 0], w["ffn_w_up"][i, 0], w["ffn_w_down"][i, 0], _row(post[0]))
        if kind == 0:
            uv = _gmlp_in(x, _row(pre[1]), w["a_w_in"][slot], _row(w["a_ln_g"][slot]), _row(w["a_ln_b"][slot]))
            x = _gmlp_out(x, uv, w["a_w_s"][slot], w["a_b_s_t"][slot], w["a_w_out"][slot], _row(post[1]))
        elif kind == 1:
            if rot is None:
                rot = _rotary_tables(seq)
            qv, kt = _qkv(x, _row(pre[1]), w["b_w_qkv"][slot], rot, batch, seq)
            o = _diffattn(qv, kt, _row(w["b_lam_q1"][slot]), _row(w["b_lam_k1"][slot]), _row(w["b_lam_q2"][slot]),
                          _row(w["b_lam_k2"][slot]), _row(w["b_subln"][slot]), batch, seq, i)
            x = _proj_res(x, o, w["b_w_out"][slot], _row(post[1]))
        else:
            if pos_table is None:
                pos_table = _pos_factor_tables(seq)
            ab = _fourier_chan(x, _row(pre[1]), w["chan_cs"], batch, seq)
            mixed = _fourier_pos(pos_table, ab, batch, seq)
            x = _proj_res(x, mixed, w["c_w_out"][slot], _row(post[1]))
        kv = _norm_mm(mem, _row(w["ln_mem"][i]), w["x_w_kv"][i])
        x = _xattn(x, _row(pre[2]), w["x_w_q"][i], kv, w["x_w_out"][i], _row(post[2]), seq)
        x = _ffn(x, _row(pre[3]), w["ffn_w_gate"][i, 1], w["ffn_w_up"][i, 1], w["ffn_w_down"][i, 1], _row(post[3]))
    return x.reshape(batch, seq, D)


def kernel(x_prompt, x_sample, mem_prompt, mem_sample, ln_pre, ln_post, ln_mem, ffn_w_gate, ffn_w_up, ffn_w_down, a_w_in, a_ln_g, a_ln_b, a_w_s, a_b_s, a_w_out, b_w_qkv, b_lam_q1, b_lam_k1, b_lam_q2, b_lam_k2, b_subln, b_w_out, c_w_out, x_w_q, x_w_kv, x_w_out):
    params = {
        "ln_pre": ln_pre, "ln_post": ln_post, "ln_mem": ln_mem,
        "ffn_w_gate": ffn_w_gate, "ffn_w_up": ffn_w_up, "ffn_w_down": ffn_w_down,
        "a_w_in": a_w_in, "a_ln_g": a_ln_g, "a_ln_b": a_ln_b, "a_w_s": a_w_s, "a_b_s": a_b_s, "a_w_out": a_w_out,
        "b_w_qkv": b_w_qkv, "b_lam_q1": b_lam_q1, "b_lam_k1": b_lam_k1, "b_lam_q2": b_lam_q2,
        "b_lam_k2": b_lam_k2, "b_subln": b_subln, "b_w_out": b_w_out,
        "c_w_out": c_w_out, "x_w_q": x_w_q, "x_w_kv": x_w_kv, "x_w_out": x_w_out,
    }
    depth = ln_pre.shape[0]
    w = _prepare(params)
    return (_trunk(x_prompt, mem_prompt, w, depth), _trunk(x_sample, mem_sample, w, depth))
```

```python
import functools
import math

import jax
import jax.numpy as jnp
from jax import lax
from jax.experimental import pallas as pl
from jax.experimental.pallas import tpu as pltpu

F32 = jnp.float32
BF16 = jnp.bfloat16

D = 2048
D_FF = 5504
LANE = 128
FF_PAD = 5632
CHUNK = 128
A_GROUPS = 8
A_GROUP_DIM = D // A_GROUPS
B_HEADS = 8
B_HEAD_DIM = 128
B_VALUE_DIM = 256
ROT_DIM = 32
ROPE_THETA = 500000.0
C_GROUPS = 8
C_GROUP_DIM = D // C_GROUPS
X_HEADS = 4
X_HEAD_DIM = 128
X_WIDTH = X_HEADS * X_HEAD_DIM
N_MEM = 256
NORM_EPS = 1e-6
SUBLN_EPS = 1e-5
LN_EPS = 1e-5
MIB = 1024 * 1024
VMEM_CAP_MIB = 60

FFN_TM, FFN_TF = 1024, 512
FFN_PIECES = 8
ROW_TM = 512
ROW_CHUNK = 128
NORM_CHUNK = ROW_TM
COL_BLOCK = 512
ATT_TQ = 512
ATT_SCORE_MIB = 32
ATT_TILE = 128 * 1024
ATT_UNROLL = 8
Q_SCALE = B_HEAD_DIM ** -0.5 * math.log2(math.e)
DFT_TM, DFT_TK = 512, 2048


def _params(semantics, vmem_mib, flags=None):
    return pltpu.CompilerParams(dimension_semantics=semantics, flags=flags,
                                vmem_limit_bytes=min(vmem_mib, VMEM_CAP_MIB) * MIB)


def _row_chunks(n_rows, fn, rc=None):
    rc = min(rc or ROW_CHUNK, n_rows)

    def body(c, carry):
        fn(pl.ds(pl.multiple_of(c * rc, rc), rc))
        return carry

    lax.fori_loop(0, n_rows // rc, body, 0)


def _rms(x, g, eps):
    return x * lax.rsqrt(jnp.mean(x * x, axis=-1, keepdims=True) + eps) * g


def _norm_rows_to(h_ref, x_ref, g_ref):
    def fn(rs):
        h_ref[rs, :] = _rms(x_ref[rs, :], g_ref[...], NORM_EPS).astype(BF16)
    _row_chunks(x_ref.shape[0], fn, NORM_CHUNK)


def _postnorm_residual_to(o_ref, x_ref, y_ref, g_ref, scale):
    def fn(rs):
        o_ref[rs, :] = x_ref[rs, :] + scale * _rms(y_ref[rs, :], g_ref[...], NORM_EPS)
    _row_chunks(x_ref.shape[0], fn, NORM_CHUNK)


def _ffn_kernel(xn_ref, xp_ref, gpre_ref, wg_ref, wu_ref, wd_ref, gpost_ref, o_ref, h_ref, acc_ref,
                *, n_tiles, pieces):
    i = pl.program_id(0)
    f = pl.program_id(1)
    side = i % 2
    main = 1 - side
    pr = xn_ref.shape[0]
    rows = pl.ds(pl.multiple_of(jnp.minimum(f, pieces - 1) * pr, pr), pr)

    def norm_in():
        h_ref[side, rows, :] = _rms(xn_ref[...], gpre_ref[...], NORM_EPS).astype(BF16)

    def finish():
        o_ref[...] = xp_ref[...] + 0.5 * _rms(acc_ref[side, rows, :], gpost_ref[...], NORM_EPS)

    def step(first, with_norms):
        h = h_ref[main]
        gate = jnp.dot(h, wg_ref[...], preferred_element_type=F32)
        up = jnp.dot(h, wu_ref[...], preferred_element_type=F32)
        a = (gate * jax.nn.sigmoid(gate) * up).astype(BF16)
        d = jnp.dot(a, wd_ref[...], preferred_element_type=F32)
        if first:
            acc_ref[main] = d
        else:
            acc_ref[main] += d
        if with_norms:
            norm_in()
            finish()

    has_piece = f < pieces

    @pl.when(jnp.logical_and(i == 0, has_piece))
    def _():
        norm_in()

        @pl.when(f == 0)
        def _():
            acc_ref[...] = jnp.zeros_like(acc_ref)

    in_main = jnp.logical_and(i >= 1, i <= n_tiles)

    @pl.when(jnp.logical_and(in_main, f == 0))
    def _():
        step(True, True)

    @pl.when(jnp.logical_and(in_main, jnp.logical_and(f > 0, has_piece)))
    def _():
        step(False, True)

    @pl.when(jnp.logical_and(in_main, f >= pieces))
    def _():
        step(False, False)

    @pl.when(jnp.logical_and(i == n_tiles + 1, has_piece))
    def _():
        finish()


def _ffn(x, g_pre, wg, wu, wd, g_post):
    n = x.shape[0]
    tm = min(FFN_TM, n)
    tf = FFN_TF
    fp = wg.shape[1]
    n_tiles, nf = n // tm, fp // tf
    pieces = FFN_PIECES
    pr = tm // pieces
    assert pieces <= nf and pr % 16 == 0

    def piece(tile, f):
        return (jnp.where(tile < 0, 0, jnp.minimum(tile, n_tiles - 1) * pieces + jnp.minimum(f, pieces - 1)), 0)

    def wcol(i, f):
        return jnp.where(jnp.logical_and(i >= 1, i <= n_tiles), f, 0)

    vmem = (3 * 2 * pr * D * 4 + 2 * tm * D * (2 + 4) + 2 * 3 * D * tf * 2 + 4 * tm * tf * 4) // MIB + 6
    return pl.pallas_call(
        functools.partial(_ffn_kernel, n_tiles=n_tiles, pieces=pieces),
        grid=(n_tiles + 2, nf),
        in_specs=[
            pl.BlockSpec((pr, D), lambda i, f: piece(i, f)),
            pl.BlockSpec((pr, D), lambda i, f: piece(i - 2, f)),
            pl.BlockSpec((1, D), lambda i, f: (0, 0)),
            pl.BlockSpec((D, tf), lambda i, f: (0, wcol(i, f))),
            pl.BlockSpec((D, tf), lambda i, f: (0, wcol(i, f))),
            pl.BlockSpec((tf, D), lambda i, f: (wcol(i, f), 0)),
            pl.BlockSpec((1, D), lambda i, f: (0, 0)),
        ],
        out_specs=pl.BlockSpec((pr, D), lambda i, f: piece(i - 2, f)),
        out_shape=jax.ShapeDtypeStruct((n, D), F32),
        scratch_shapes=[pltpu.VMEM((2, tm, D), BF16), pltpu.VMEM((2, tm, D), F32)],
        compiler_params=_params(("arbitrary", "arbitrary"), vmem),
        name="ffn",
    )(x, x, g_pre, wg, wu, wd, g_post)


def _norm_mm_kernel(x_ref, g_ref, w_ref, o_ref, h_ref):
    _norm_rows_to(h_ref, x_ref, g_ref)
    o_ref[...] = jnp.dot(h_ref[...], w_ref[...], preferred_element_type=F32).astype(o_ref.dtype)


def _norm_mm(x, g, w):
    n, n_out = x.shape[0], w.shape[1]
    tm = min(ROW_TM, n)
    tn = min(1024, n_out)
    vmem = (2 * tm * D * 4 + tm * D * 2 + 2 * D * tn * 2 + 2 * tm * tn * 2 + tm * tn * 4) // MIB + 6
    return pl.pallas_call(
        _norm_mm_kernel,
        grid=(n_out // tn, n // tm),
        in_specs=[
            pl.BlockSpec((tm, D), lambda j, i: (i, 0)),
            pl.BlockSpec((1, D), lambda j, i: (0, 0)),
            pl.BlockSpec((D, tn), lambda j, i: (0, j)),
        ],
        out_specs=pl.BlockSpec((tm, tn), lambda j, i: (i, j)),
        out_shape=jax.ShapeDtypeStruct((n, n_out), BF16),
        scratch_shapes=[pltpu.VMEM((tm, D), BF16)],
        compiler_params=_params(("parallel", "parallel"), vmem),
        name="norm_mm",
    )(x, g, w)


def _proj_res_kernel(x_ref, a_ref, w_ref, g_ref, o_ref, y_ref):
    y_ref[...] = jnp.dot(a_ref[...], w_ref[...], preferred_element_type=F32)
    _postnorm_residual_to(o_ref, x_ref, y_ref, g_ref, 1.0)


def _proj_res(x, a, w, g_post):
    n, k = a.shape
    tm = min(ROW_TM, n)
    vmem = (2 * 2 * tm * D * 4 + 2 * tm * k * 2 + 2 * k * D * 2 + tm * D * 4) // MIB + 6
    return pl.pallas_call(
        _proj_res_kernel,
        grid=(n // tm,),
        in_specs=[
            pl.BlockSpec((tm, D), lambda i: (i, 0)),
            pl.BlockSpec((tm, k), lambda i: (i, 0)),
            pl.BlockSpec((k, D), lambda i: (0, 0)),
            pl.BlockSpec((1, D), lambda i: (0, 0)),
        ],
        out_specs=pl.BlockSpec((tm, D), lambda i: (i, 0)),
        out_shape=jax.ShapeDtypeStruct((n, D), F32),
        scratch_shapes=[pltpu.VMEM((tm, D), F32)],
        compiler_params=_params(("parallel",), vmem),
        name="proj_res",
    )(x, a, w, g_post)


def _gelu(y):
    return 0.5 * y * (1.0 + lax.erf(y * (0.5 ** 0.5)))


def _gmlp_in_kernel(x_ref, g_ref, w_ref, lg_ref, lb_ref, o_ref, h_ref, y_ref):
    j = pl.program_id(0)
    _norm_rows_to(h_ref, x_ref, g_ref)

    def gelu_block(c):
        cs = slice(c * COL_BLOCK, (c + 1) * COL_BLOCK)
        return cs, _gelu(jnp.dot(h_ref[...], w_ref[:, cs], preferred_element_type=F32))

    @pl.when(j == 0)
    def _():
        for c in range(D // COL_BLOCK):
            cs, y = gelu_block(c)
            o_ref[:, cs] = y.astype(BF16)

    @pl.when(j == 1)
    def _():
        for c in range(D // COL_BLOCK):
            cs, y = gelu_block(c)
            y_ref[:, cs] = y

        def fn(rs):
            v = y_ref[rs, :]
            vc = v - jnp.mean(v, axis=-1, keepdims=True)
            var = jnp.mean(vc * vc, axis=-1, keepdims=True)
            o_ref[rs, :] = (vc * lax.rsqrt(var + LN_EPS) * lg_ref[...] + lb_ref[...]).astype(BF16)
        _row_chunks(x_ref.shape[0], fn, NORM_CHUNK)


def _gmlp_in(x, g_pre, w_in, ln_g, ln_b):
    n = x.shape[0]
    tm = min(ROW_TM, n)
    vmem = (2 * tm * D * 4 + tm * D * 2 + 2 * D * D * 2 + 2 * tm * D * 2 + tm * D * 4) // MIB + 6
    return pl.pallas_call(
        _gmlp_in_kernel,
        grid=(2, n // tm),
        in_specs=[
            pl.BlockSpec((tm, D), lambda j, i: (i, 0)),
            pl.BlockSpec((1, D), lambda j, i: (0, 0)),
            pl.BlockSpec((D, D), lambda j, i: (0, j)),
            pl.BlockSpec((1, D), lambda j, i: (0, 0)),
            pl.BlockSpec((1, D), lambda j, i: (0, 0)),
        ],
        out_specs=pl.BlockSpec((None, tm, D), lambda j, i: (j, i, 0)),
        out_shape=jax.ShapeDtypeStruct((2, n, D), BF16),
        scratch_shapes=[pltpu.VMEM((tm, D), BF16), pltpu.VMEM((tm, D), F32)],
        compiler_params=_params(("parallel", "parallel"), vmem),
        name="gmlp_in",
    )(x, g_pre, w_in, ln_g, ln_b)


def _gmlp_out_kernel(x_ref, u_ref, v_ref, ws_ref, bs_ref, w_ref, g_ref, o_ref, a_ref, y_ref):
    for c in range(x_ref.shape[0] // CHUNK):
        rs = slice(c * CHUNK, (c + 1) * CHUNK)
        for g in range(A_GROUPS):
            cs = slice(g * A_GROUP_DIM, (g + 1) * A_GROUP_DIM)
            mixed = jnp.dot(ws_ref[g], v_ref[rs, cs], preferred_element_type=F32) + bs_ref[:, g:g + 1]
            a_ref[rs, cs] = (u_ref[rs, cs].astype(F32) * mixed).astype(BF16)
    y_ref[...] = jnp.dot(a_ref[...], w_ref[...], preferred_element_type=F32)
    _postnorm_residual_to(o_ref, x_ref, y_ref, g_ref, 1.0)


def _gmlp_out(x, uv, w_s, b_s_t, w_out, g_post):
    n = x.shape[0]
    tm = min(ROW_TM, n)
    assert ROW_CHUNK == CHUNK and tm % CHUNK == 0
    vmem = (2 * 2 * tm * D * 4 + 2 * 2 * tm * D * 2 + 2 * D * D * 2 + tm * D * (2 + 4)) // MIB + 6
    return pl.pallas_call(
        _gmlp_out_kernel,
        grid=(n // tm,),
        in_specs=[
            pl.BlockSpec((tm, D), lambda i: (i, 0)),
            pl.BlockSpec((None, tm, D), lambda i: (0, i, 0)),
            pl.BlockSpec((None, tm, D), lambda i: (1, i, 0)),
            pl.BlockSpec((A_GROUPS, CHUNK, CHUNK), lambda i: (0, 0, 0)),
            pl.BlockSpec((CHUNK, A_GROUPS), lambda i: (0, 0)),
            pl.BlockSpec((D, D), lambda i: (0, 0)),
            pl.BlockSpec((1, D), lambda i: (0, 0)),
        ],
        out_specs=pl.BlockSpec((tm, D), lambda i: (i, 0)),
        out_shape=jax.ShapeDtypeStruct((n, D), F32),
        scratch_shapes=[pltpu.VMEM((tm, D), BF16), pltpu.VMEM((tm, D), F32)],
        compiler_params=_params(("parallel",), vmem),
        name="gmlp_out",
    )(x, uv, uv, w_s, b_s_t, w_out, g_post)


def _rotary_tables(s):
    half = ROT_DIM // 2
    inv_freq = 1.0 / (ROPE_THETA ** (jnp.arange(0, ROT_DIM, 2, dtype=F32) / ROT_DIM))
    ang = jnp.arange(s, dtype=F32)[:, None] * inv_freq[None, :]
    cos, sin = jnp.cos(ang), jnp.sin(ang)
    rest = B_HEAD_DIM - ROT_DIM
    ct = jnp.concatenate([cos, cos, jnp.ones((s, rest), F32)], axis=1)
    sa = jnp.concatenate([-sin, jnp.zeros((s, half + rest), F32)], axis=1)
    sb = jnp.concatenate([jnp.zeros((s, half), F32), sin, jnp.zeros((s, rest), F32)], axis=1)
    return ct, sa, sb


def _qkv_kernel(x_ref, g_ref, w_ref, ct_ref, sa_ref, sb_ref, qv_ref, kt_ref, h_ref, t_ref):
    j = pl.program_id(0)
    _norm_rows_to(h_ref, x_ref, g_ref)

    def block(c):
        return jnp.dot(h_ref[...], w_ref[:, c * COL_BLOCK:(c + 1) * COL_BLOCK], preferred_element_type=F32)

    def rotary_blocks(scale, write):
        t_ref[0] = ct_ref[...] * scale
        t_ref[1] = sa_ref[...] * scale
        t_ref[2] = sb_ref[...] * scale
        for c in range(D // COL_BLOCK):
            y = block(c)
            for k in range(COL_BLOCK // LANE):
                yc = y[:, k * LANE:(k + 1) * LANE]
                rot = (yc * t_ref[0] + pltpu.roll(yc, LANE - ROT_DIM // 2, 1) * t_ref[1]
                       + pltpu.roll(yc, ROT_DIM // 2, 1) * t_ref[2])
                write(c * COL_BLOCK + k * LANE, rot)

    @pl.when(j == 0)
    def _():
        def write(lo, rot):
            qv_ref[:, lo:lo + LANE] = rot.astype(BF16)
        rotary_blocks(Q_SCALE, write)

    @pl.when(j == 1)
    def _():
        def write(lo, rot):
            rt = rot.T.astype(BF16)
            tk = kt_ref.shape[2]
            for s in range(kt_ref.shape[0]):
                kt_ref[s, lo:lo + LANE, :] = rt[:, s * tk:(s + 1) * tk]
        rotary_blocks(1.0, write)

    @pl.when(j == 2)
    def _():
        for c in range(D // COL_BLOCK):
            qv_ref[:, c * COL_BLOCK:(c + 1) * COL_BLOCK] = block(c).astype(BF16)


def _att_tiles(seq):
    tq = min(ATT_TQ, seq, ATT_SCORE_MIB * MIB // (8 * seq))
    return tq, min(ATT_TILE // tq, seq, ROW_TM)


def _qkv(x, g_pre, w_qkv, tables, batch, seq):
    n = x.shape[0]
    tm = min(ROW_TM, seq)
    tk = _att_tiles(seq)[1]
    sub = tm // tk
    pos_tiles = seq // tm
    tiles = n // tm
    ct, sa, sb = tables
    vmem = (2 * tm * D * 4 + tm * D * 2 + 2 * D * D * 2 + 4 * tm * D * 2 + tm * COL_BLOCK * 4) // MIB + 8
    tab = pl.BlockSpec((tm, LANE), lambda j, i: (i % pos_tiles, 0))

    def qv_block(j, i):
        return (j // 2, jnp.where(j == 1, tiles - 1, i), 0)

    def kt_block(j, i):
        t = jnp.where(j == 0, 0, jnp.where(j == 1, i, tiles - 1))
        return (t // pos_tiles, t % pos_tiles, 0, 0)

    return pl.pallas_call(
        _qkv_kernel,
        grid=(3, tiles),
        in_specs=[
            pl.BlockSpec((tm, D), lambda j, i: (i, 0)),
            pl.BlockSpec((1, D), lambda j, i: (0, 0)),
            pl.BlockSpec((D, D), lambda j, i: (0, j)),
            tab, tab, tab,
        ],
        out_specs=[pl.BlockSpec((None, tm, D), qv_block),
                   pl.BlockSpec((None, sub, D, tk), kt_block)],
        out_shape=[jax.ShapeDtypeStruct((2, n, D), BF16),
                   jax.ShapeDtypeStruct((batch, seq // tk, D, tk), BF16)],
        scratch_shapes=[pltpu.VMEM((tm, D), BF16), pltpu.VMEM((3, tm, LANE), F32)],
        compiler_params=_params(("arbitrary", "arbitrary"), vmem),
        name="qkv",
    )(x, g_pre, w_qkv, ct, sa, sb)


def _diffattn_kernel(q_ref, kt_ref, v_ref, lq1_ref, lk1_ref, lq2_ref, lk2_ref, sg_ref, o_ref,
                     s_ref, m_ref, l_ref, p_ref, acc_ref, *, tk, lam_init):
    seq = v_ref.shape[0]
    lane_blocks = tk // LANE
    unroll = max(1, min(ATT_UNROLL, seq // tk // 2))
    lam = (jnp.exp(jnp.sum(lq1_ref[...] * lk1_ref[...], axis=-1, keepdims=True))
           - jnp.exp(jnp.sum(lq2_ref[...] * lk2_ref[...], axis=-1, keepdims=True)) + lam_init)
    m_ref[...] = jnp.full_like(m_ref, -jnp.inf)
    l_ref[...] = jnp.zeros_like(l_ref)
    acc_ref[...] = jnp.zeros_like(acc_ref)

    def scores(t, carry):
        for c in range(2):
            cs = slice(c * B_HEAD_DIM, (c + 1) * B_HEAD_DIM)
            s = jnp.dot(q_ref[:, cs], kt_ref[t, cs, :], preferred_element_type=F32)
            s_ref[c, t] = s
            m = m_ref[c]
            for j in range(lane_blocks):
                m = jnp.maximum(m, s[:, j * LANE:(j + 1) * LANE])
            m_ref[c] = m
        return carry

    lax.fori_loop(0, seq // tk, scores, 0, unroll=unroll)
    for c in range(2):
        m_ref[c] = jnp.broadcast_to(jnp.max(m_ref[c], axis=-1, keepdims=True), m_ref.shape[1:])

    def weights(t, carry):
        ks = pl.ds(pl.multiple_of(t * tk, tk), tk)
        for c in range(2):
            m = m_ref[c]
            l = l_ref[c]
            for j in range(lane_blocks):
                ls = slice(j * LANE, (j + 1) * LANE)
                p = jnp.exp2(s_ref[c, t, :, ls] - m)
                l = l + p
                p_ref[c, :, ls] = p.astype(BF16)
            l_ref[c] = l
            acc_ref[c] += jnp.dot(p_ref[c], v_ref[ks, :], preferred_element_type=F32)
        return carry

    lax.fori_loop(0, seq // tk, weights, 0, unroll=unroll)
    l0 = jnp.sum(l_ref[0], axis=-1, keepdims=True)
    l1 = jnp.sum(l_ref[1], axis=-1, keepdims=True)
    o = acc_ref[0] / l0 - lam * (acc_ref[1] / l1)
    o_ref[...] = (_rms(o, sg_ref[...], SUBLN_EPS) * (1.0 - lam_init)).astype(BF16)


def _diffattn(qv, kt, lam_q1, lam_k1, lam_q2, lam_k2, subln_g, batch, seq, layer_idx):
    n = batch * seq
    tq, tk = _att_tiles(seq)
    q_tiles = seq // tq
    hw = 2 * B_HEAD_DIM
    lam_init = 0.8 - 0.6 * math.exp(-0.3 * layer_idx)
    vec = pl.BlockSpec((1, B_HEAD_DIM), lambda b, h, i: (0, 0))
    vmem = (2 * 2 * seq * hw * 2 + 4 * tq * hw * 2 + 2 * tq * seq * 4 + 2 * tq * hw * 4 + 4 * tq * tk * 4) // MIB + 8
    return pl.pallas_call(
        functools.partial(_diffattn_kernel, tk=tk, lam_init=lam_init),
        grid=(batch, B_HEADS, q_tiles),
        in_specs=[
            pl.BlockSpec((None, tq, hw), lambda b, h, i: (0, b * q_tiles + i, h)),
            pl.BlockSpec((None, seq // tk, hw, tk), lambda b, h, i: (b, 0, h, 0)),
            pl.BlockSpec((None, seq, hw), lambda b, h, i: (1, b, h)),
            vec, vec, vec, vec,
            pl.BlockSpec((1, B_VALUE_DIM), lambda b, h, i: (0, 0)),
        ],
        out_specs=pl.BlockSpec((tq, hw), lambda b, h, i: (b * q_tiles + i, h)),
        out_shape=jax.ShapeDtypeStruct((n, D), BF16),
        scratch_shapes=[pltpu.VMEM((2, seq // tk, tq, tk), F32),
                        pltpu.VMEM((2, tq, LANE), F32), pltpu.VMEM((2, tq, LANE), F32),
                        pltpu.VMEM((2, tq, tk), BF16), pltpu.VMEM((2, tq, B_VALUE_DIM), F32)],
        compiler_params=_params(("parallel", "parallel", "arbitrary"), vmem),
        name="diffattn",
    )(qv, kt, qv, lam_q1, lam_k1, lam_q2, lam_k2, subln_g)


def _dft_tables(n):
    idx = jnp.arange(n, dtype=jnp.int32)
    ang = ((idx[:, None] * idx[None, :]) % n).astype(F32) * (2.0 * math.pi / n)
    return jnp.cos(ang), jnp.sin(ang)


def _fourier_chan_kernel(x_ref, g_ref, cs_ref, o_ref, h_ref):
    _norm_rows_to(h_ref, x_ref, g_ref)
    for g in range(C_GROUPS):
        cs = slice(g * C_GROUP_DIM, (g + 1) * C_GROUP_DIM)
        ab = jnp.dot(h_ref[:, cs], cs_ref[...], preferred_element_type=F32)
        o_ref[0, :, cs] = ab[:, :C_GROUP_DIM].astype(BF16)
        o_ref[1, :, cs] = ab[:, C_GROUP_DIM:].astype(BF16)


def _fourier_chan(x, g_pre, chan_cs, batch, seq):
    tm = min(ROW_TM, seq)
    tiles = seq // tm
    vmem = (2 * tm * D * 4 + tm * D * 2 + 2 * 2 * tm * D * 2) // MIB + 8
    return pl.pallas_call(
        _fourier_chan_kernel,
        grid=(batch, tiles),
        in_specs=[
            pl.BlockSpec((tm, D), lambda b, i: (b * tiles + i, 0)),
            pl.BlockSpec((1, D), lambda b, i: (0, 0)),
            pl.BlockSpec((C_GROUP_DIM, 2 * C_GROUP_DIM), lambda b, i: (0, 0)),
        ],
        out_specs=pl.BlockSpec((None, 2, tm, D), lambda b, i: (b, 0, i, 0)),
        out_shape=jax.ShapeDtypeStruct((batch, 2, seq, D), BF16),
        scratch_shapes=[pltpu.VMEM((tm, D), BF16)],
        compiler_params=_params(("parallel", "parallel"), vmem),
        name="fourier_chan",
    )(x, g_pre, chan_cs)


def _pos_factor_tables(seq):
    k = jnp.arange(seq, dtype=jnp.int32)

    def cos_sin(j):
        ang = ((j[:, None] * k[None, :]) % seq).astype(F32) * (2.0 * math.pi / seq)
        return jnp.cos(ang), jnp.sin(ang)

    ca, sa = cos_sin(jnp.arange(seq // LANE, dtype=jnp.int32) * LANE)
    cb, sb = cos_sin(jnp.arange(LANE, dtype=jnp.int32))
    return (jnp.concatenate([ca, -sa], axis=1), jnp.concatenate([-sa, -ca], axis=1),
            jnp.concatenate([cb, cb], axis=1), jnp.concatenate([sb, sb], axis=1))


def _fourier_pos_kernel(p_ref, q_ref, u_ref, v_ref, ab_ref, lo_ref, hi_ref, t_ref, acc_ref, *, norm, groups):
    i = pl.program_id(1)
    k = pl.program_id(2)
    half = pl.num_programs(2) // 2

    @pl.when(k == 0)
    def _():
        acc_ref[...] = jnp.zeros_like(acc_ref)

    first = (i % (p_ref.shape[0] // groups)) * groups
    for g in range(groups):
        t_ref[g * LANE:(g + 1) * LANE, :] = (p_ref[pl.ds(first + g, 1), :] * u_ref[...]
                                             + q_ref[pl.ds(first + g, 1), :] * v_ref[...]).astype(BF16)
    acc_ref[k // half] += jnp.dot(t_ref[...], ab_ref[...], preferred_element_type=F32)

    @pl.when(k == pl.num_programs(2) - 1)
    def _():
        def fn(rs):
            c, s = acc_ref[0, rs, :], acc_ref[1, rs, :]
            lo_ref[rs, :] = ((c + s) * norm).astype(BF16)
            hi_ref[rs, :] = ((c - s) * norm).astype(BF16)
        _row_chunks(t_ref.shape[0], fn)


def _fourier_assemble_kernel(lo_ref, hia_ref, hib_ref, o_ref, stack_ref, *, lower_tiles):
    i = pl.program_id(1)
    tm = o_ref.shape[0]

    @pl.when(i < lower_tiles)
    def _():
        o_ref[...] = lo_ref[...]

    @pl.when(i >= lower_tiles)
    def _():
        stack_ref[0:tm, :] = hia_ref[...]
        stack_ref[tm:, :] = hib_ref[...]
        row = lax.broadcasted_iota(jnp.int32, (tm, 2 * tm), 0)
        col = lax.broadcasted_iota(jnp.int32, (tm, 2 * tm), 1)
        backwards = (col == tm - row).astype(BF16)
        o_ref[...] = jnp.dot(backwards, stack_ref[...], preferred_element_type=F32).astype(BF16)


def _fourier_assemble(lo, hi, batch, seq, tm):
    half_tiles = seq // 2 // tm
    src_tiles = half_tiles + 1
    all_tiles = seq // tm

    def lo_block(b, i):
        return (b * src_tiles + jnp.minimum(i, half_tiles - 1), 0)

    def hi_block(offset):
        return lambda b, i: (b * src_tiles + jnp.clip(all_tiles - i - 1, 0, half_tiles - 1) + offset, 0)

    return pl.pallas_call(
        functools.partial(_fourier_assemble_kernel, lower_tiles=half_tiles),
        grid=(batch, all_tiles),
        in_specs=[pl.BlockSpec((tm, D), lo_block),
                  pl.BlockSpec((tm, D), hi_block(0)),
                  pl.BlockSpec((tm, D), hi_block(1))],
        out_specs=pl.BlockSpec((tm, D), lambda b, i: (b * all_tiles + i, 0)),
        out_shape=jax.ShapeDtypeStruct((batch * seq, D), BF16),
        scratch_shapes=[pltpu.VMEM((2 * tm, D), BF16)],
        compiler_params=_params(("parallel", "parallel"), 8 * tm * D * 2 // MIB + 8),
        name="fourier_assemble",
    )(lo, hi, hi)


def _fourier_pos(tables, ab, batch, seq):
    half = seq // 2
    tm = min(DFT_TM, half)
    tk = min(DFT_TK, seq)
    tiles = half // tm + 1
    ab2 = ab.reshape(batch * 2 * seq, D)
    k_tiles = 2 * seq // tk
    groups = tm // LANE
    window = max(groups, 8)
    vmem = (tm * tk * 2 + 4 * LANE * tk * 4 + 2 * tk * D * 2 + 2 * tm * D * 4 + 4 * tm * D * 2) // MIB + 8
    coarse = pl.BlockSpec((window, tk), lambda b, i, k: (i * groups // window, k))
    fine = pl.BlockSpec((LANE, tk), lambda b, i, k: (0, k))
    out = pl.BlockSpec((tm, D), lambda b, i, k: (b * tiles + i, 0))
    rows = jax.ShapeDtypeStruct((batch * tiles * tm, D), BF16)
    lo, hi = pl.pallas_call(
        functools.partial(_fourier_pos_kernel, norm=(seq * C_GROUP_DIM) ** -0.5, groups=groups),
        grid=(batch, tiles, k_tiles),
        in_specs=[coarse, coarse, fine, fine,
                  pl.BlockSpec((tk, D), lambda b, i, k: (b * k_tiles + k, 0))],
        out_specs=[out, out],
        out_shape=[rows, rows],
        scratch_shapes=[pltpu.VMEM((tm, tk), BF16), pltpu.VMEM((2, tm, D), F32)],
        compiler_params=_params(("parallel", "parallel", "arbitrary"), vmem),
        name="fourier_pos",
    )(*tables, ab2)
    return _fourier_assemble(lo, hi, batch, seq, tm)


def _xattn_kernel(x_ref, gpre_ref, wq_ref, kv_ref, wo_ref, gpost_ref, o_ref, h_ref, q_ref, a_ref, y_ref):
    _norm_rows_to(h_ref, x_ref, gpre_ref)
    q_ref[...] = (jnp.dot(h_ref[...], wq_ref[...], preferred_element_type=F32)
                  * (X_HEAD_DIM ** -0.5)).astype(BF16)

    for h in range(X_HEADS):
        cs = slice(h * X_HEAD_DIM, (h + 1) * X_HEAD_DIM)
        vs = slice(X_WIDTH + h * X_HEAD_DIM, X_WIDTH + (h + 1) * X_HEAD_DIM)
        s = lax.dot_general(q_ref[:, cs], kv_ref[:, cs], (((1,), (1,)), ((), ())),
                            preferred_element_type=F32)
        e = jnp.exp(s - jnp.max(s, axis=-1, keepdims=True))
        pv = jnp.dot(e.astype(BF16), kv_ref[:, vs], preferred_element_type=F32)
        a_ref[:, cs] = (pv / jnp.sum(e, axis=-1, keepdims=True)).astype(BF16)
    y_ref[...] = jnp.dot(a_ref[...], wo_ref[...], preferred_element_type=F32)
    _postnorm_residual_to(o_ref, x_ref, y_ref, gpost_ref, 1.0)


def _xattn(x, g_pre, w_q, kv, w_out, g_post, seq):
    n = x.shape[0]
    tm = min(ROW_TM, seq)
    tiles = seq // tm
    vmem = (2 * 2 * tm * D * 4 + tm * D * (2 + 4) + 4 * D * X_WIDTH * 2 + 2 * tm * X_WIDTH * 2) // MIB + 8
    return pl.pallas_call(
        _xattn_kernel,
        grid=(n // tm,),
        in_specs=[
            pl.BlockSpec((tm, D), lambda i: (i, 0)),
            pl.BlockSpec((1, D), lambda i: (0, 0)),
            pl.BlockSpec((D, X_WIDTH), lambda i: (0, 0)),
            pl.BlockSpec((N_MEM, 2 * X_WIDTH), lambda i: (i // tiles, 0)),
            pl.BlockSpec((X_WIDTH, D), lambda i: (0, 0)),
            pl.BlockSpec((1, D), lambda i: (0, 0)),
        ],
        out_specs=pl.BlockSpec((tm, D), lambda i: (i, 0)),
        out_shape=jax.ShapeDtypeStruct((n, D), F32),
        scratch_shapes=[pltpu.VMEM((tm, D), BF16), pltpu.VMEM((tm, X_WIDTH), BF16),
                        pltpu.VMEM((tm, X_WIDTH), BF16), pltpu.VMEM((tm, D), F32)],
        compiler_params=_params(("parallel",), vmem),
        name="xattn",
    )(x, g_pre, w_q, kv, w_out, g_post)


def _prepare(p):
    pad = FF_PAD - D_FF
    w = dict(p)
    w["ffn_w_gate"] = jnp.pad(p["ffn_w_gate"].astype(BF16), ((0, 0), (0, 0), (0, 0), (0, pad)))
    w["ffn_w_up"] = jnp.pad(p["ffn_w_up"].astype(BF16), ((0, 0), (0, 0), (0, 0), (0, pad)))
    w["ffn_w_down"] = jnp.pad(p["ffn_w_down"].astype(BF16), ((0, 0), (0, 0), (0, pad), (0, 0)))
    for name in ("a_w_in", "a_w_s", "a_w_out", "b_w_qkv", "b_w_out", "c_w_out", "x_w_q", "x_w_kv", "x_w_out"):
        w[name] = p[name].astype(BF16)
    w["a_b_s_t"] = jnp.swapaxes(p["a_b_s"], 1, 2)
    cc, sc = _dft_tables(C_GROUP_DIM)
    w["chan_cs"] = jnp.concatenate([cc, sc], axis=1).astype(BF16)
    return w


def _row(v):
    return v.reshape(1, -1)


def _trunk(x3, mem3, w, depth):
    batch, seq, _ = x3.shape
    x = x3.reshape(batch * seq, D)
    mem = mem3.reshape(batch * N_MEM, D)
    n_mixers = 3
    rot = pos_table = None
    for i in range(depth):
        kind, slot = i % n_mixers, i // n_mixers
        pre, post = w["ln_pre"][i], w["ln_post"][i]
        x = _ffn(x, _row(pre[0]), w["ffn_w_gate"][i, 0], w["ffn_w_up"][i, 0], w["ffn_w_down"][i, 0], _row(post[0]))
        if kind == 0:
            uv = _gmlp_in(x, _row(pre[1]), w["a_w_in"][slot], _row(w["a_ln_g"][slot]), _row(w["a_ln_b"][slot]))
            x = _gmlp_out(x, uv, w["a_w_s"][slot], w["a_b_s_t"][slot], w["a_w_out"][slot], _row(post[1]))
        elif kind == 1:
            if rot is None:
                rot = _rotary_tables(seq)
            qv, kt = _qkv(x, _row(pre[1]), w["b_w_qkv"][slot], rot, batch, seq)
            o = _diffattn(qv, kt, _row(w["b_lam_q1"][slot]), _row(w["b_lam_k1"][slot]), _row(w["b_lam_q2"][slot]),
                          _row(w["b_lam_k2"][slot]), _row(w["b_subln"][slot]), batch, seq, i)
            x = _proj_res(x, o, w["b_w_out"][slot], _row(post[1]))
        else:
            if pos_table is None:
                pos_table = _pos_factor_tables(seq)
            ab = _fourier_chan(x, _row(pre[1]), w["chan_cs"], batch, seq)
            mixed = _fourier_pos(pos_table, ab, batch, seq)
            x = _proj_res(x, mixed, w["c_w_out"][slot], _row(post[1]))
        kv = _norm_mm(mem, _row(w["ln_mem"][i]), w["x_w_kv"][i])
        x = _xattn(x, _row(pre[2]), w["x_w_q"][i], kv, w["x_w_out"][i], _row(post[2]), seq)
        x = _ffn(x, _row(pre[3]), w["ffn_w_gate"][i, 1], w["ffn_w_up"][i, 1], w["ffn_w_down"][i, 1], _row(post[3]))
    return x.reshape(batch, seq, D)


def kernel(x_prompt, x_sample, mem_prompt, mem_sample, ln_pre, ln_post, ln_mem, ffn_w_gate, ffn_w_up, ffn_w_down, a_w_in, a_ln_g, a_ln_b, a_w_s, a_b_s, a_w_out, b_w_qkv, b_lam_q1, b_lam_k1, b_lam_q2, b_lam_k2, b_subln, b_w_out, c_w_out, x_w_q, x_w_kv, x_w_out):
    params = {
        "ln_pre": ln_pre, "ln_post": ln_post, "ln_mem": ln_mem,
        "ffn_w_gate": ffn_w_gate, "ffn_w_up": ffn_w_up, "ffn_w_down": ffn_w_down,
        "a_w_in": a_w_in, "a_ln_g": a_ln_g, "a_ln_b": a_ln_b, "a_w_s": a_w_s, "a_b_s": a_b_s, "a_w_out": a_w_out,
        "b_w_qkv": b_w_qkv, "b_lam_q1": b_lam_q1, "b_lam_k1": b_lam_k1, "b_lam_q2": b_lam_q2,
        "b_lam_k2": b_lam_k2, "b_subln": b_subln, "b_w_out": b_w_out,
        "c_w_out": c_w_out, "x_w_q": x_w_q, "x_w_kv": x_w_kv, "x_w_out": x_w_out,
    }
    depth = ln_pre.shape[0]
    w = _prepare(params)
    return (_trunk(x_prompt, mem_prompt, w, depth), _trunk(x_sample, mem_sample, w, depth))
```
